```python
import math
import numpy as np
import jax, jax.numpy as jnp
from jax import lax

D_MODEL = 1024
BATCH = 2
SEQ = 8192
DEPTH = 2

GRID_W = 64
MEM_LEN = 256
HEAD_DIM = 64
DA_HEADS = 4
DA_VDIM = 2 * HEAD_DIM
DA_QK = DA_HEADS * 2 * HEAD_DIM
DA_V = DA_HEADS * DA_VDIM
Q_BLOCK = 128
NA_HEADS = 8
NA_W = NA_HEADS * HEAD_DIM
NA_WIN_ROWS = 8
NA_WIN_COLS = 16
NA_QCOLS = 16
NA_KCOLS = 32
SC_W = 512
SC_K = 3
N_BRANCH = 3
BRANCH_W = 512
IN_SPLITS = (DA_QK, DA_QK, DA_V, NA_W, NA_W, NA_W, SC_W, SC_W, SC_W, N_BRANCH * D_MODEL)
IN_COLS = 7680
XA_HEADS = 4
XA_DIM = D_MODEL // XA_HEADS
D_FF = 2816
FFN_K = 3
ROPE_THETA = 10000.0
LN_EPS = 1e-5

kernel_name = 'hybrid_diff_natten_shortconv_encoder'


def layer_norm(x, g, b):
    xf = x.astype(jnp.float32)
    mu = jnp.mean(xf, axis=-1, keepdims=True)
    var = jnp.mean(jnp.square(xf - mu), axis=-1, keepdims=True)
    y = (xf - mu) * lax.rsqrt(var + LN_EPS)
    return (y * g.astype(jnp.float32) + b.astype(jnp.float32)).astype(x.dtype)


def rms_norm(x, g):
    xf = x.astype(jnp.float32)
    y = xf * lax.rsqrt(jnp.mean(jnp.square(xf), axis=-1, keepdims=True) + LN_EPS)
    return (y * g.astype(jnp.float32)).astype(x.dtype)


def rope(x, pos):
    d = x.shape[-1]
    half = d // 2
    inv = ROPE_THETA ** (-jnp.arange(half, dtype=jnp.float32) * 2.0 / d)
    ang = pos.astype(jnp.float32)[:, None] * inv[None, :]
    cos = jnp.cos(ang).astype(x.dtype)
    sin = jnp.sin(ang).astype(x.dtype)
    x1, x2 = x[..., :half], x[..., half:]
    return jnp.concatenate([x1 * cos - x2 * sin, x2 * cos + x1 * sin], axis=-1)


def dwconv(x, w):
    k, c = w.shape
    return lax.conv_general_dilated(x, w[:, None, :], window_strides=(1,), padding=[(k // 2, k // 2)],
                                    dimension_numbers=('NWC', 'WIO', 'NWC'), feature_group_count=c)


def diff_attention(q, k, v, lam, pos):
    b, s = q.shape[0], q.shape[1]
    q = rope(q.transpose(0, 2, 3, 1, 4), pos) * (HEAD_DIM ** -0.5)
    k = rope(k.transpose(0, 2, 3, 1, 4), pos)
    vt = v.transpose(0, 2, 1, 3)
    nb = s // Q_BLOCK
    qb = q.reshape(b, DA_HEADS, 2, nb, Q_BLOCK, HEAD_DIM).transpose(3, 0, 1, 2, 4, 5)

    def block(qblk):
        sc = jnp.einsum('bhmqd,bhmkd->bhmqk', qblk, k, preferred_element_type=jnp.float32)
        p = jax.nn.softmax(sc, axis=-1)
        w = p[:, :, 0] - lam * p[:, :, 1]
        return jnp.einsum('bhqk,bhkv->bhqv', w.astype(vt.dtype), vt)

    o = lax.map(block, qb)
    return o.transpose(1, 2, 0, 3, 4).reshape(b, DA_HEADS, s, DA_VDIM)


def _na_static(rows):
    wr = min(NA_WIN_ROWS, rows)
    n_cb = GRID_W // NA_QCOLS
    c = np.arange(GRID_W).reshape(n_cb, NA_QCOLS)
    cs = np.clip(c - NA_WIN_COLS // 2, 0, GRID_W - NA_WIN_COLS)
    kstart = np.minimum(cs[:, 0], GRID_W - NA_KCOLS)
    kcol = kstart[:, None] + np.arange(NA_KCOLS)
    rel = kcol[:, None, :] - c[:, :, None]
    valid = (kcol[:, None, :] >= cs[:, :, None]) & (kcol[:, None, :] < cs[:, :, None] + NA_WIN_COLS)
    dc_idx = np.clip(rel + NA_WIN_COLS - 1, 0, 2 * NA_WIN_COLS - 2)
    return wr, n_cb, kcol.astype(np.int32), dc_idx.astype(np.int32), valid


def neighbourhood_attention(q, k, v, rpb):
    b, s = q.shape[0], q.shape[1]
    rows = s // GRID_W
    wr, n_cb, kcol, dc_idx, valid = _na_static(rows)

    def grid(t):
        return t.reshape(b, rows, GRID_W, NA_HEADS, HEAD_DIM).transpose(0, 3, 1, 2, 4)

    qg = grid(q) * (HEAD_DIM ** -0.5)
    kg = grid(k)
    vg = grid(v)
    qr = qg.reshape(b, NA_HEADS, rows, n_cb, NA_QCOLS, HEAD_DIM).transpose(2, 0, 1, 3, 4, 5)
    kcol_j = jnp.asarray(kcol)
    mask = jnp.asarray(valid)[:, :, None, :]
    bias_c = rpb[:, :, jnp.asarray(dc_idx)]

    def row_block(args):
        r, qblk = args
        rs = jnp.clip(r - wr // 2, 0, rows - wr)
        k_rows = lax.dynamic_slice_in_dim(kg, rs, wr, axis=2)
        v_rows = lax.dynamic_slice_in_dim(vg, rs, wr, axis=2)
        k_blk = jnp.take(k_rows, kcol_j, axis=3)
        v_blk = jnp.take(v_rows, kcol_j, axis=3)
        sc = jnp.einsum('bhjqd,bhrjkd->bhjqrk', qblk, k_blk, preferred_element_type=jnp.float32)
        dr_idx = rs + jnp.arange(wr) - r + (NA_WIN_ROWS - 1)
        bias = jnp.take(bias_c, dr_idx, axis=1).transpose(0, 2, 3, 1, 4)
        sc = jnp.where(mask, sc + bias[None].astype(jnp.float32), -jnp.inf)
        p = jax.nn.softmax(sc.reshape(sc.shape[:4] + (wr * NA_KCOLS,)), axis=-1).reshape(sc.shape)
        return jnp.einsum('bhjqrk,bhrjkd->bhjqd', p.astype(v_blk.dtype), v_blk)

    o = lax.map(row_block, (jnp.arange(rows, dtype=jnp.int32), qr))
    return o.transpose(1, 0, 3, 4, 2, 5).reshape(b, s, NA_W)


def hybrid_mixer(x, w_in, lam_q1, lam_k1, lam_q2, lam_k2, subln_g, rpb, sc_conv_w, w_branch, w_mix_out, lam_init, pos):
    b, s, _ = x.shape
    proj = x @ w_in
    qa, ka, va, qb, kb, vb, gb, gc, hc, gates = jnp.split(proj, np.cumsum(IN_SPLITS)[:-1].tolist(), axis=-1)
    lam = (jnp.exp(jnp.sum(lam_q1.astype(jnp.float32) * lam_k1.astype(jnp.float32)))
           - jnp.exp(jnp.sum(lam_q2.astype(jnp.float32) * lam_k2.astype(jnp.float32))) + lam_init)
    oa = diff_attention(qa.reshape(b, s, DA_HEADS, 2, HEAD_DIM), ka.reshape(b, s, DA_HEADS, 2, HEAD_DIM),
                        va.reshape(b, s, DA_HEADS, DA_VDIM), lam, pos)
    ya = (rms_norm(oa, subln_g) * (1.0 - lam_init)).transpose(0, 2, 1, 3).reshape(b, s, DA_V)
    yb = neighbourhood_attention(qb, kb, vb, rpb)
    yc = gb * dwconv(gc * hc, sc_conv_w)
    g = jax.nn.sigmoid(gates.reshape(b, s, N_BRANCH, D_MODEL))
    merged = (g[:, :, 0] * (ya @ w_branch[0]) + g[:, :, 1] * (yb @ w_branch[1])
              + g[:, :, 2] * (yc @ w_branch[2]))
    return merged @ w_mix_out


def memory_cross_attention(x, mem, xa_q, xa_kv, xa_o):
    b, s, _ = x.shape
    q = (x @ xa_q).reshape(b, s, XA_HEADS, XA_DIM) * (XA_DIM ** -0.5)
    k, v = jnp.split(mem @ xa_kv, 2, axis=-1)
    k = k.reshape(b, mem.shape[1], XA_HEADS, XA_DIM)
    v = v.reshape(b, mem.shape[1], XA_HEADS, XA_DIM)
    p = jax.nn.softmax(jnp.einsum('bqhd,bkhd->bhqk', q, k, preferred_element_type=jnp.float32), axis=-1)
    o = jnp.einsum('bhqk,bkhd->bqhd', p.astype(v.dtype), v).reshape(b, s, D_MODEL)
    return o @ xa_o


def conv_ffn(x, ffn_w_in, ffn_conv_w, ffn_conv_b, ffn_w_out):
    u, gt = jnp.split(x @ ffn_w_in, 2, axis=-1)
    a = dwconv(gt, ffn_conv_w) + ffn_conv_b
    return (jax.nn.silu(a) * u) @ ffn_w_out


def setup_inputs(seed: int = 0) -> dict:
    key = jax.random.key(seed)
    ks = jax.random.split(key, 24)
    beta = (8.0 * DEPTH) ** -0.25

    def nrm(k, shape, scale):
        return jax.random.normal(k, shape, jnp.float32) * scale

    is_value = (False, False, True, False, False, True, False, False, True, False)
    col_scale = np.concatenate([np.full(n, beta if iv else 1.0, np.float32) for n, iv in zip(IN_SPLITS, is_value)])
    kv_scale = np.concatenate([np.ones(D_MODEL, np.float32), np.full(D_MODEL, beta, np.float32)])
    return {
        'x': nrm(ks[0], (BATCH, SEQ, D_MODEL), 1.0),
        'mem': nrm(ks[1], (BATCH, MEM_LEN, D_MODEL), 1.0),
        'emb_ln_g': 1.0 + nrm(ks[2], (D_MODEL,), 0.02),
        'emb_ln_b': nrm(ks[3], (D_MODEL,), 0.02),
        'w_in': nrm(ks[4], (DEPTH, D_MODEL, IN_COLS), D_MODEL ** -0.5) * jnp.asarray(col_scale),
        'lam_q1': nrm(ks[5], (DEPTH, HEAD_DIM), 0.1),
        'lam_k1': nrm(ks[6], (DEPTH, HEAD_DIM), 0.1),
        'lam_q2': nrm(ks[7], (DEPTH, HEAD_DIM), 0.1),
        'lam_k2': nrm(ks[8], (DEPTH, HEAD_DIM), 0.1),
        'subln_g': 1.0 + nrm(ks[9], (DEPTH, DA_VDIM), 0.02),
        'rpb': nrm(ks[10], (DEPTH, NA_HEADS, 2 * NA_WIN_ROWS - 1, 2 * NA_WIN_COLS - 1), 0.1),
        'sc_conv_w': nrm(ks[11], (DEPTH, SC_K, SC_W), SC_K ** -0.5),
        'w_branch': nrm(ks[12], (DEPTH, N_BRANCH, BRANCH_W, D_MODEL), BRANCH_W ** -0.5 * beta),
        'w_mix_out': nrm(ks[13], (DEPTH, D_MODEL, D_MODEL), D_MODEL ** -0.5 * beta),
        'xa_q': nrm(ks[14], (DEPTH, D_MODEL, D_MODEL), D_MODEL ** -0.5),
        'xa_kv': nrm(ks[15], (DEPTH, D_MODEL, 2 * D_MODEL), D_MODEL ** -0.5) * jnp.asarray(kv_scale),
        'xa_o': nrm(ks[16], (DEPTH, D_MODEL, D_MODEL), D_MODEL ** -0.5 * beta),
        'ffn_w_in': nrm(ks[17], (DEPTH, D_MODEL, 2 * D_FF), D_MODEL ** -0.5 * beta),
        'ffn_conv_w': nrm(ks[18], (DEPTH, FFN_K, D_FF), FFN_K ** -0.5),
        'ffn_conv_b': nrm(ks[19], (DEPTH, D_FF), 0.02),
        'ffn_w_out': nrm(ks[20], (DEPTH, D_FF, D_MODEL), D_FF ** -0.5 * beta),
        'ln_g': 1.0 + nrm(ks[21], (DEPTH, 3, D_MODEL), 0.02),
        'ln_b': nrm(ks[22], (DEPTH, 3, D_MODEL), 0.02),
    }


def reference(x, mem, emb_ln_g, emb_ln_b, w_in, lam_q1, lam_k1, lam_q2, lam_k2, subln_g, rpb, sc_conv_w,
              w_branch, w_mix_out, xa_q, xa_kv, xa_o, ffn_w_in, ffn_conv_w, ffn_conv_b, ffn_w_out, ln_g, ln_b):
    alpha = (2.0 * DEPTH) ** 0.25
    pos = jnp.arange(x.shape[1], dtype=jnp.int32)
    x = layer_norm(x, emb_ln_g, emb_ln_b)
    for l in range(DEPTH):
        lam_init = 0.8 - 0.6 * math.exp(-0.3 * l)
        h = hybrid_mixer(x, w_in[l], lam_q1[l], lam_k1[l], lam_q2[l], lam_k2[l], subln_g[l], rpb[l],
                         sc_conv_w[l], w_branch[l], w_mix_out[l], lam_init, pos)
        x = layer_norm(alpha * x + h, ln_g[l, 0], ln_b[l, 0])
        h = memory_cross_attention(x, mem, xa_q[l], xa_kv[l], xa_o[l])
        x = layer_norm(alpha * x + h, ln_g[l, 1], ln_b[l, 1])
        h = conv_ffn(x, ffn_w_in[l], ffn_conv_w[l], ffn_conv_b[l], ffn_w_out[l])
        x = layer_norm(alpha * x + h, ln_g[l, 2], ln_b[l, 2])
    return x
```

```python
import functools
import math

import numpy as np
import jax
import jax.numpy as jnp
from jax import lax
from jax.experimental import pallas as pl
from jax.experimental.pallas import tpu as pltpu

GRID_W = 64
HEAD_DIM = 64
DA_HEADS = 4
DA_VDIM = 2 * HEAD_DIM
NA_HEADS = 8
NA_WIN_ROWS = 8
NA_WIN_COLS = 16
SC_W = 512
N_BRANCH = 3
XA_HEADS = 4
ROPE_THETA = 10000.0
LN_EPS = 1e-5

LANES = 128
MXU_DTYPE = jnp.bfloat16
ACT_DTYPE = jnp.bfloat16
NEG_BIG = -1e30
VMEM_LIMIT = 56 * 1024 * 1024


def _params(*sem):
    return pltpu.CompilerParams(dimension_semantics=sem, vmem_limit_bytes=VMEM_LIMIT)


def _resident(shape, index_map):
    return pl.BlockSpec(shape, index_map, pipeline_mode=pl.Buffered(1))


def _layer_norm(v, g, b):
    mu = jnp.mean(v, axis=-1, keepdims=True)
    d = v - mu
    var = jnp.mean(d * d, axis=-1, keepdims=True)
    return d * lax.rsqrt(var + LN_EPS) * g + b


def _dot(a, b):
    return jnp.dot(a.astype(MXU_DTYPE), b.astype(MXU_DTYPE), preferred_element_type=jnp.float32)


def _dot_nt(a, b):
    return lax.dot_general(a.astype(MXU_DTYPE), b.astype(MXU_DTYPE), (((1,), (1,)), ((), ())),
                           preferred_element_type=jnp.float32)


def _ln_kernel(x_ref, g_ref, b_ref, o_ref):
    o_ref[...] = _layer_norm(x_ref[...], g_ref[...], b_ref[...])


def _embed_ln(x2d, g, b, tm):
    t, d = x2d.shape
    return pl.pallas_call(
        _ln_kernel,
        out_shape=jax.ShapeDtypeStruct((t, d), jnp.float32),
        grid=(t // tm,),
        in_specs=[pl.BlockSpec((tm, d), lambda i: (i, 0)),
                  pl.BlockSpec((1, d), lambda i: (0, 0)),
                  pl.BlockSpec((1, d), lambda i: (0, 0))],
        out_specs=pl.BlockSpec((tm, d), lambda i: (i, 0)),
        compiler_params=_params("parallel"),
        name="embed_ln",
    )(x2d, g.reshape(1, d), b.reshape(1, d))


def _proj_kernel(x_ref, w_ref, rope_ref, qa_ref, ka_ref, va_ref, qb_ref, kb_ref, vb_ref,
                 gb_ref, u_ref, gate_ref, *, d_gate):
    xb = x_ref[...].astype(MXU_DTYPE)
    tm = xb.shape[0]
    w = SC_W

    def mm(c0, c1):
        return jnp.dot(xb, w_ref[:, c0:c1], preferred_element_type=jnp.float32)

    lane = lax.broadcasted_iota(jnp.int32, (tm, LANES), 1)
    first_half = (lane % HEAD_DIM) < (HEAD_DIM // 2)

    def rope(y, cos, sin):
        outs = []
        for c in range(y.shape[1] // LANES):
            yc = y[:, c * LANES:(c + 1) * LANES]
            rot = jnp.where(first_half, pltpu.roll(yc, LANES - HEAD_DIM // 2, 1),
                            pltpu.roll(yc, HEAD_DIM // 2, 1))
            outs.append(yc * cos + rot * sin)
        return jnp.concatenate(outs, axis=1)

    tab = rope_ref[...]
    qa_ref[...] = rope(mm(0, w), tab[:, 0:LANES], tab[:, LANES:2 * LANES]).astype(qa_ref.dtype)
    ka_ref[...] = rope(mm(w, 2 * w), tab[:, 2 * LANES:3 * LANES], tab[:, 3 * LANES:4 * LANES]).astype(ka_ref.dtype)
    va_ref[...] = mm(2 * w, 3 * w).astype(va_ref.dtype)
    qb_ref[...] = (mm(3 * w, 4 * w) * (HEAD_DIM ** -0.5)).astype(qb_ref.dtype)
    kb_ref[...] = mm(4 * w, 5 * w).astype(kb_ref.dtype)
    vb_ref[...] = mm(5 * w, 6 * w).astype(vb_ref.dtype)
    gb_ref[...] = mm(6 * w, 7 * w).astype(gb_ref.dtype)
    u_ref[...] = (mm(7 * w, 8 * w) * mm(8 * w, 9 * w)).astype(u_ref.dtype)
    gate_ref[...] = jax.nn.sigmoid(mm(9 * w, 9 * w + d_gate)).astype(gate_ref.dtype)


def _input_proj(x2d, w_in, rope_tab, seq, tm):
    t, d = x2d.shape
    n_cols = w_in.shape[1]
    d_gate = n_cols - 9 * SC_W
    s_blocks = seq // tm
    row = lambda i: (i, 0)
    small = jax.ShapeDtypeStruct((t, SC_W), ACT_DTYPE)
    return pl.pallas_call(
        functools.partial(_proj_kernel, d_gate=d_gate),
        out_shape=[small] * 8 + [jax.ShapeDtypeStruct((t, d_gate), ACT_DTYPE)],
        grid=(t // tm,),
        in_specs=[pl.BlockSpec((tm, d), row),
                  _resident((d, n_cols), lambda i: (0, 0)),
                  pl.BlockSpec((tm, 4 * LANES), lambda i: (i % s_blocks, 0))],
        out_specs=[pl.BlockSpec((tm, SC_W), row)] * 8 + [pl.BlockSpec((tm, d_gate), row)],
        compiler_params=_params("parallel"),
        name="input_proj",
    )(x2d, w_in, rope_tab)


def _rope_table(seq):
    half = HEAD_DIM // 2
    inv = ROPE_THETA ** (-jnp.arange(half, dtype=jnp.float32) * 2.0 / HEAD_DIM)
    ang = jnp.arange(seq, dtype=jnp.float32)[:, None] * inv[None, :]
    cos, sin = jnp.cos(ang), jnp.sin(ang)
    cos_l = jnp.tile(jnp.concatenate([cos, cos], axis=1), (1, LANES // HEAD_DIM))
    sin_l = jnp.tile(jnp.concatenate([-sin, sin], axis=1), (1, LANES // HEAD_DIM))
    scale = HEAD_DIM ** -0.5
    return jnp.concatenate([cos_l * scale, sin_l * scale, cos_l, sin_l], axis=1)


def _diff_attn_kernel(q_ref, k_ref, v_ref, lq1_ref, lk1_ref, lq2_ref, lk2_ref, g_ref, o_ref,
                      qs_ref, m_ref, acc_ref, *, tq, tk, n_kv, lam_init):
    lane = lax.broadcasted_iota(jnp.int32, (tq, LANES), 1)
    q = q_ref[...]
    zero = jnp.zeros_like(q)
    qs_ref[0:tq, :] = jnp.where(lane < HEAD_DIM, q, zero)
    qs_ref[tq:2 * tq, :] = jnp.where(lane >= HEAD_DIM, q, zero)
    m_ref[...] = jnp.full(m_ref.shape, NEG_BIG, jnp.float32)
    acc_ref[...] = jnp.zeros(acc_ref.shape, jnp.float32)
    ones_col = jnp.where(lax.broadcasted_iota(jnp.int32, (tk, LANES), 1) == 0, 1.0, 0.0).astype(v_ref.dtype)

    def body(j, carry):
        off = pl.multiple_of(j * tk, tk)
        k = k_ref[pl.ds(off, tk), :]
        v = v_ref[pl.ds(off, tk), :]
        s = _dot_nt(qs_ref[...], k)
        m_prev = m_ref[...]
        m_new = jnp.maximum(m_prev, jnp.max(s, axis=1, keepdims=True))
        p = jnp.exp(s - jnp.concatenate([m_new] * (tk // LANES), axis=1))
        alpha = jnp.exp(m_prev - m_new)
        v_ext = jnp.concatenate([v, ones_col], axis=1)
        pv = _dot(p, v_ext)
        acc_ref[...] = acc_ref[...] * jnp.concatenate([alpha, alpha], axis=1) + pv
        m_ref[...] = m_new
        return carry

    lax.fori_loop(0, n_kv, body, 0)

    acc = acc_ref[...]
    o = acc[:, 0:LANES] / acc[:, LANES:LANES + 1]
    lam = (jnp.exp(jnp.sum(lq1_ref[...] * lk1_ref[...], axis=1, keepdims=True))
           - jnp.exp(jnp.sum(lq2_ref[...] * lk2_ref[...], axis=1, keepdims=True)) + lam_init)
    od = o[0:tq] - lam * o[tq:2 * tq]
    ms = jnp.mean(od * od, axis=-1, keepdims=True)
    y = od * lax.rsqrt(ms + LN_EPS) * g_ref[...] * (1.0 - lam_init)
    o_ref[...] = y.astype(o_ref.dtype)


def _diff_attention(qa, ka, va, lq1, lk1, lq2, lk2, subln_g, lam_init, batch, seq, tq, tk):
    t = qa.shape[0]
    n_q = seq // tq
    vec = lambda a: a.reshape(1, -1)
    vspec = lambda n: pl.BlockSpec((1, n), lambda b, h, i: (0, 0))
    return pl.pallas_call(
        functools.partial(_diff_attn_kernel, tq=tq, tk=tk, n_kv=seq // tk, lam_init=lam_init),
        out_shape=jax.ShapeDtypeStruct((t, DA_HEADS * DA_VDIM), ACT_DTYPE),
        grid=(batch, DA_HEADS, n_q),
        in_specs=[pl.BlockSpec((tq, LANES), lambda b, h, i: (b * n_q + i, h)),
                  pl.BlockSpec((seq, LANES), lambda b, h, i: (b, h)),
                  pl.BlockSpec((seq, LANES), lambda b, h, i: (b, h)),
                  vspec(HEAD_DIM), vspec(HEAD_DIM), vspec(HEAD_DIM), vspec(HEAD_DIM), vspec(DA_VDIM)],
        out_specs=pl.BlockSpec((tq, LANES), lambda b, h, i: (b * n_q + i, h)),
        scratch_shapes=[pltpu.VMEM((2 * tq, LANES), qa.dtype),
                        pltpu.VMEM((2 * tq, LANES), jnp.float32),
                        pltpu.VMEM((2 * tq, 2 * LANES), jnp.float32)],
        compiler_params=_params("parallel", "parallel", "parallel"),
        name="diff_attention",
    )(qa, ka, va, vec(lq1), vec(lk1), vec(lq2), vec(lk2), vec(subln_g))


def _na_bias_table(rpb):
    qc = np.arange(GRID_W)
    cs = np.clip(qc - NA_WIN_COLS // 2, 0, GRID_W - NA_WIN_COLS)
    kc = np.arange(GRID_W)
    valid = (kc[None, :] >= cs[:, None]) & (kc[None, :] < cs[:, None] + NA_WIN_COLS)
    dc = np.clip(kc[None, :] - qc[:, None] + NA_WIN_COLS - 1, 0, 2 * NA_WIN_COLS - 2)
    kind = np.arange(NA_WIN_ROWS)
    wi = np.arange(NA_WIN_ROWS)
    dr = wi[None, :] + (NA_WIN_ROWS - 1) - kind[:, None]
    tab = rpb[:, dr[:, :, None, None], dc[None, None, :, :]]
    tab = jnp.where(jnp.asarray(valid)[None, None, None], tab, NEG_BIG)
    tab = tab.transpose(0, 1, 3, 2, 4)
    return tab.reshape(rpb.shape[0], NA_WIN_ROWS, GRID_W, NA_WIN_ROWS * GRID_W).astype(jnp.float32)


def _na_kernel(q_ref, k_ref, v_ref, bias_ref, o_ref, *, rows, rows_per_step):
    i = pl.program_id(2)
    win = NA_WIN_ROWS * GRID_W
    lane = lax.broadcasted_iota(jnp.int32, (GRID_W, LANES), 1)
    lo = lane < HEAD_DIM
    for rr in range(rows_per_step):
        r = i * rows_per_step + rr
        rs = jnp.clip(r - NA_WIN_ROWS // 2, 0, rows - NA_WIN_ROWS)
        kind = jnp.where(r < NA_WIN_ROWS // 2, r,
                         jnp.where(r > rows - NA_WIN_ROWS // 2, r - (rows - NA_WIN_ROWS), NA_WIN_ROWS // 2))
        off = pl.multiple_of(rs * GRID_W, GRID_W)
        kwin = k_ref[pl.ds(off, win), :]
        vwin = v_ref[pl.ds(off, win), :]
        q = q_ref[rr * GRID_W:(rr + 1) * GRID_W, :]
        zero = jnp.zeros_like(q)
        qst = jnp.concatenate([jnp.where(lo, q, zero), jnp.where(lo, zero, q)], axis=0)
        bias = jnp.concatenate([bias_ref[0, kind], bias_ref[1, kind]], axis=0)
        s = _dot_nt(qst, kwin) + bias
        m = jnp.max(s, axis=1, keepdims=True)
        p = jnp.exp(s - m)
        l = jnp.sum(p, axis=1, keepdims=True)
        pv = _dot(p, vwin) / l
        o = jnp.where(lo, pv[0:GRID_W], pv[GRID_W:2 * GRID_W])
        o_ref[rr * GRID_W:(rr + 1) * GRID_W, :] = o.astype(o_ref.dtype)


def _neighbourhood_attention(qb, kb, vb, bias_tab, batch, seq, rows_per_step):
    t = qb.shape[0]
    rows = seq // GRID_W
    tq = rows_per_step * GRID_W
    n_q = seq // tq
    pairs = NA_HEADS * HEAD_DIM // LANES
    return pl.pallas_call(
        functools.partial(_na_kernel, rows=rows, rows_per_step=rows_per_step),
        out_shape=jax.ShapeDtypeStruct((t, NA_HEADS * HEAD_DIM), ACT_DTYPE),
        grid=(batch, pairs, n_q),
        in_specs=[pl.BlockSpec((tq, LANES), lambda b, h, i: (b * n_q + i, h)),
                  pl.BlockSpec((seq, LANES), lambda b, h, i: (b, h)),
                  pl.BlockSpec((seq, LANES), lambda b, h, i: (b, h)),
                  pl.BlockSpec((2, NA_WIN_ROWS, GRID_W, NA_WIN_ROWS * GRID_W), lambda b, h, i: (h, 0, 0, 0))],
        out_specs=pl.BlockSpec((tq, LANES), lambda b, h, i: (b * n_q + i, h)),
        compiler_params=_params("parallel", "parallel", "parallel"),
        name="neighbourhood_attention",
    )(qb, kb, vb, bias_tab)


def _shift_rows(a, prev_row, next_row):
    n = a.shape[0]
    ridx = lax.broadcasted_iota(jnp.int32, a.shape, 0)
    up = jnp.where(ridx == 0, prev_row, pltpu.roll(a, 1, 0))
    dn = jnp.where(ridx == n - 1, next_row, pltpu.roll(a, n - 1, 0))
    return up, dn


def _merge_kernel(x_ref, ya_ref, yb_ref, gb_ref, u_ref, up_ref, un_ref, gate_ref, cw_ref, wb_ref, wm_ref,
                  g_ref, b_ref, o_ref, *, s_blocks, halo, alpha, d_model):
    i = pl.program_id(0)
    has_prev = i % s_blocks != 0
    has_next = i % s_blocks != s_blocks - 1
    u = u_ref[...].astype(jnp.float32)
    prev_row = jnp.where(has_prev, up_ref[halo - 1:halo, :].astype(jnp.float32), 0.0)
    next_row = jnp.where(has_next, un_ref[0:1, :].astype(jnp.float32), 0.0)
    u_up, u_dn = _shift_rows(u, prev_row, next_row)
    cw = cw_ref[...]
    conv = u_up * cw[0:1, :] + u * cw[1:2, :] + u_dn * cw[2:3, :]
    yc = gb_ref[...].astype(jnp.float32) * conv
    gate = gate_ref[...].astype(jnp.float32)
    merged = (gate[:, 0:d_model] * _dot(ya_ref[...], wb_ref[0])
              + gate[:, d_model:2 * d_model] * _dot(yb_ref[...], wb_ref[1])
              + gate[:, 2 * d_model:3 * d_model] * _dot(yc, wb_ref[2]))
    h = _dot(merged, wm_ref[...])
    o_ref[...] = _layer_norm(alpha * x_ref[...] + h, g_ref[...], b_ref[...])


def _merge(x2d, ya, yb, gb, u, gates, conv_w, w_branch, w_mix, ln_g, ln_b, seq, tm, alpha):
    t, d = x2d.shape
    halo = 16
    hb = tm // halo
    n_halo = t // halo
    row = lambda i: (i, 0)
    const2 = lambda i: (0, 0)
    return pl.pallas_call(
        functools.partial(_merge_kernel, s_blocks=seq // tm, halo=halo, alpha=alpha, d_model=d),
        out_shape=jax.ShapeDtypeStruct((t, d), jnp.float32),
        grid=(t // tm,),
        in_specs=[pl.BlockSpec((tm, d), row),
                  pl.BlockSpec((tm, SC_W), row),
                  pl.BlockSpec((tm, SC_W), row),
                  pl.BlockSpec((tm, SC_W), row),
                  pl.BlockSpec((tm, SC_W), row),
                  pl.BlockSpec((halo, SC_W), lambda i: (jnp.maximum(i * hb - 1, 0), 0)),
                  pl.BlockSpec((halo, SC_W), lambda i: (jnp.minimum((i + 1) * hb, n_halo - 1), 0)),
                  pl.BlockSpec((tm, N_BRANCH * d), row),
                  pl.BlockSpec(conv_w.shape, const2),
                  _resident(w_branch.shape, lambda i: (0, 0, 0)),
                  _resident(w_mix.shape, const2),
                  pl.BlockSpec((1, d), const2),
                  pl.BlockSpec((1, d), const2)],
        out_specs=pl.BlockSpec((tm, d), row),
        compiler_params=_params("parallel"),
        name="branch_merge",
    )(x2d, ya, yb, gb, u, u, u, gates, conv_w, w_branch, w_mix, ln_g.reshape(1, d), ln_b.reshape(1, d))


def _kv_kernel(m_ref, w_ref, o_ref):
    o_ref[...] = _dot(m_ref[...], w_ref[...]).astype(o_ref.dtype)


def _mem_kv(mem2d, xa_kv):
    t, d = mem2d.shape
    n = xa_kv.shape[1]
    tm = min(t, 256)
    return pl.pallas_call(
        _kv_kernel,
        out_shape=jax.ShapeDtypeStruct((t, n), ACT_DTYPE),
        grid=(t // tm,),
        in_specs=[pl.BlockSpec((tm, d), lambda i: (i, 0)), _resident((d, n), lambda i: (0, 0))],
        out_specs=pl.BlockSpec((tm, n), lambda i: (i, 0)),
        compiler_params=_params("parallel"),
        name="mem_kv",
    )(mem2d, xa_kv)


def _xattn_kernel(x_ref, k_ref, v_ref, wq_ref, wo_ref, g_ref, b_ref, o_ref, *, alpha):
    x = x_ref[...]
    d = x.shape[1]
    hd = d // XA_HEADS
    q = _dot(x, wq_ref[...]) * (hd ** -0.5)
    k = k_ref[...]
    v = v_ref[...]
    outs = []
    for h in range(XA_HEADS):
        sl = slice(h * hd, (h + 1) * hd)
        s = _dot_nt(q[:, sl], k[:, sl])
        m = jnp.max(s, axis=1, keepdims=True)
        p = jnp.exp(s - m)
        l = jnp.sum(p, axis=1, keepdims=True)
        outs.append(_dot(p, v[:, sl]) / l)
    o = jnp.concatenate(outs, axis=1)
    h_out = _dot(o, wo_ref[...])
    o_ref[...] = _layer_norm(alpha * x + h_out, g_ref[...], b_ref[...])


def _cross_attention(x2d, kv, xa_q, xa_o, ln_g, ln_b, seq, mem_len, tm, alpha):
    t, d = x2d.shape
    s_blocks = seq // tm
    row = lambda i: (i, 0)
    const2 = lambda i: (0, 0)
    return pl.pallas_call(
        functools.partial(_xattn_kernel, alpha=alpha),
        out_shape=jax.ShapeDtypeStruct((t, d), jnp.float32),
        grid=(t // tm,),
        in_specs=[pl.BlockSpec((tm, d), row),
                  pl.BlockSpec((mem_len, d), lambda i: (i // s_blocks, 0)),
                  pl.BlockSpec((mem_len, d), lambda i: (i // s_blocks, 1)),
                  _resident((d, d), const2),
                  _resident((d, d), const2),
                  pl.BlockSpec((1, d), const2),
                  pl.BlockSpec((1, d), const2)],
        out_specs=pl.BlockSpec((tm, d), row),
        compiler_params=_params("parallel"),
        name="cross_attention",
    )(x2d, kv, kv, xa_q, xa_o, ln_g.reshape(1, d), ln_b.reshape(1, d))


def _ffn_kernel(x_ref, xp_ref, xn_ref, wi_ref, cw_ref, cb_ref, wo_ref, g_ref, b_ref, o_ref, acc_ref,
                *, s_blocks, halo, alpha, d_ff, tf):
    i = pl.program_id(0)
    has_prev = i % s_blocks != 0
    has_next = i % s_blocks != s_blocks - 1
    x = x_ref[...]
    xb = x.astype(MXU_DTYPE)
    xe = jnp.concatenate([jnp.where(has_prev, xp_ref[halo - 1:halo, :], 0.0),
                          jnp.where(has_next, xn_ref[0:1, :], 0.0),
                          jnp.zeros((halo - 2, x.shape[1]), jnp.float32)], axis=0).astype(MXU_DTYPE)
    for c in range(d_ff // tf):
        w_u = wi_ref[:, c * tf:(c + 1) * tf]
        w_g = wi_ref[:, d_ff + c * tf:d_ff + (c + 1) * tf]
        u = jnp.dot(xb, w_u, preferred_element_type=jnp.float32)
        gt = jnp.dot(xb, w_g, preferred_element_type=jnp.float32)
        ge = jnp.dot(xe, w_g, preferred_element_type=jnp.float32)
        g_up, g_dn = _shift_rows(gt, ge[0:1, :], ge[1:2, :])
        cw = cw_ref[:, c * tf:(c + 1) * tf]
        a = g_up * cw[0:1, :] + gt * cw[1:2, :] + g_dn * cw[2:3, :] + cb_ref[:, c * tf:(c + 1) * tf]
        hidden = (a * jax.nn.sigmoid(a)) * u
        part = _dot(hidden, wo_ref[c * tf:(c + 1) * tf, :])
        if c == 0:
            acc_ref[...] = part
        else:
            acc_ref[...] += part
    o_ref[...] = _layer_norm(alpha * x + acc_ref[...], g_ref[...], b_ref[...])


def _conv_ffn(x2d, w_in, conv_w, conv_b, w_out, ln_g, ln_b, seq, tm, alpha):
    t, d = x2d.shape
    d_ff = w_out.shape[0]
    tf = 256
    halo = 8
    hb = tm // halo
    n_halo = t // halo
    row = lambda i: (i, 0)
    const2 = lambda i: (0, 0)
    return pl.pallas_call(
        functools.partial(_ffn_kernel, s_blocks=seq // tm, halo=halo, alpha=alpha, d_ff=d_ff, tf=tf),
        out_shape=jax.ShapeDtypeStruct((t, d), jnp.float32),
        grid=(t // tm,),
        in_specs=[pl.BlockSpec((tm, d), row),
                  pl.BlockSpec((halo, d), lambda i: (jnp.maximum(i * hb - 1, 0), 0)),
                  pl.BlockSpec((halo, d), lambda i: (jnp.minimum((i + 1) * hb, n_halo - 1), 0)),
                  _resident(w_in.shape, const2),
                  pl.BlockSpec(conv_w.shape, const2),
                  pl.BlockSpec((1, d_ff), const2),
                  _resident(w_out.shape, const2),
                  pl.BlockSpec((1, d), const2),
                  pl.BlockSpec((1, d), const2)],
        out_specs=pl.BlockSpec((tm, d), row),
        scratch_shapes=[pltpu.VMEM((tm, d), jnp.float32)],
        compiler_params=_params("parallel"),
        name="conv_ffn",
    )(x2d, x2d, x2d, w_in, conv_w, conv_b.reshape(1, d_ff), w_out, ln_g.reshape(1, d), ln_b.reshape(1, d))


def kernel(x, mem, emb_ln_g, emb_ln_b, w_in, lam_q1, lam_k1, lam_q2, lam_k2, subln_g, rpb, sc_conv_w, w_branch,
           w_mix_out, xa_q, xa_kv, xa_o, ffn_w_in, ffn_conv_w, ffn_conv_b, ffn_w_out, ln_g, ln_b):
    batch, seq, d = x.shape
    depth = w_in.shape[0]
    mem_len = mem.shape[1]
    assert seq % GRID_W == 0 and seq // GRID_W >= NA_WIN_ROWS
    alpha = (2.0 * depth) ** 0.25
    tm = min(512, seq)
    tq = min(512, seq)
    tk = min(1024, seq)
    wcast = lambda a: a.astype(MXU_DTYPE)

    rope_tab = _rope_table(seq)
    h = _embed_ln(x.reshape(batch * seq, d), emb_ln_g, emb_ln_b, tm)
    mem2d = mem.reshape(batch * mem_len, d)
    for l in range(depth):
        lam_init = 0.8 - 0.6 * math.exp(-0.3 * l)
        qa, ka, va, qb, kb, vb, gb, u, gates = _input_proj(h, wcast(w_in[l]), rope_tab, seq, tm)
        ya = _diff_attention(qa, ka, va, lam_q1[l], lam_k1[l], lam_q2[l], lam_k2[l], subln_g[l], lam_init,
                             batch, seq, tq, tk)
        yb = _neighbourhood_attention(qb, kb, vb, _na_bias_table(rpb[l]), batch, seq, NA_WIN_ROWS)
        h = _merge(h, ya, yb, gb, u, gates, sc_conv_w[l], wcast(w_branch[l]), wcast(w_mix_out[l]),
                   ln_g[l, 0], ln_b[l, 0], seq, tm, alpha)
        kv = _mem_kv(mem2d, wcast(xa_kv[l]))
        h = _cross_attention(h, kv, wcast(xa_q[l]), wcast(xa_o[l]), ln_g[l, 1], ln_b[l, 1], seq, mem_len, tm, alpha)
        h = _conv_ffn(h, wcast(ffn_w_in[l]), ffn_conv_w[l], ffn_conv_b[l], wcast(ffn_w_out[l]),
                      ln_g[l, 2], ln_b[l, 2], seq, tm, alpha)
    return h.reshape(batch, seq, d)
```

```python
import functools
import math

import numpy as np
import jax
import jax.numpy as jnp
from jax import lax
from jax.experimental import pallas as pl
from jax.experimental.pallas import tpu as pltpu

GRID_W = 64
HEAD_DIM = 64
DA_HEADS = 4
DA_VDIM = 2 * HEAD_DIM
NA_HEADS = 8
NA_WIN_ROWS = 8
NA_WIN_COLS = 16
SC_W = 512
N_BRANCH = 3
XA_HEADS = 4
ROPE_THETA = 10000.0
LN_EPS = 1e-5

LANES = 128
MXU_DTYPE = jnp.bfloat16
ACT_DTYPE = jnp.bfloat16
NEG_BIG = -1e30
VMEM_LIMIT = 56 * 1024 * 1024


def _params(*sem):
    return pltpu.CompilerParams(dimension_semantics=sem, vmem_limit_bytes=VMEM_LIMIT)


def _resident(shape, index_map):
    return pl.BlockSpec(shape, index_map, pipeline_mode=pl.Buffered(1))


def _layer_norm(v, g, b):
    mu = jnp.mean(v, axis=-1, keepdims=True)
    d = v - mu
    var = jnp.mean(d * d, axis=-1, keepdims=True)
    return d * lax.rsqrt(var + LN_EPS) * g + b


def _dot(a, b):
    return jnp.dot(a.astype(MXU_DTYPE), b.astype(MXU_DTYPE), preferred_element_type=jnp.float32)


def _dot_nt(a, b):
    return lax.dot_general(a.astype(MXU_DTYPE), b.astype(MXU_DTYPE), (((1,), (1,)), ((), ())),
                           preferred_element_type=jnp.float32)


def _ln_kernel(x_ref, g_ref, b_ref, o_ref):
    o_ref[...] = _layer_norm(x_ref[...], g_ref[...], b_ref[...])


def _embed_ln(x2d, g, b, tm):
    t, d = x2d.shape
    return pl.pallas_call(
        _ln_kernel,
        out_shape=jax.ShapeDtypeStruct((t, d), jnp.float32),
        grid=(t // tm,),
        in_specs=[pl.BlockSpec((tm, d), lambda i: (i, 0)),
                  pl.BlockSpec((1, d), lambda i: (0, 0)),
                  pl.BlockSpec((1, d), lambda i: (0, 0))],
        out_specs=pl.BlockSpec((tm, d), lambda i: (i, 0)),
        compiler_params=_params("parallel"),
        name="embed_ln",
    )(x2d, g.reshape(1, d), b.reshape(1, d))


def _proj_kernel(x_ref, w_ref, rope_ref, qa_ref, ka_ref, va_ref, qb_ref, kb_ref, vb_ref,
                 gb_ref, u_ref, gate_ref, *, d_gate):
    xb = x_ref[...].astype(MXU_DTYPE)
    tm = xb.shape[0]
    w = SC_W

    def mm(c0, c1):
        return jnp.dot(xb, w_ref[:, c0:c1], preferred_element_type=jnp.float32)

    lane = lax.broadcasted_iota(jnp.int32, (tm, LANES), 1)
    first_half = (lane % HEAD_DIM) < (HEAD_DIM // 2)

    def rope(y, cos, sin):
        outs = []
        for c in range(y.shape[1] // LANES):
            yc = y[:, c * LANES:(c + 1) * LANES]
            rot = jnp.where(first_half, pltpu.roll(yc, LANES - HEAD_DIM // 2, 1),
                            pltpu.roll(yc, HEAD_DIM // 2, 1))
            outs.append(yc * cos + rot * sin)
        return jnp.concatenate(outs, axis=1)

    tab = rope_ref[...]
    qa_ref[...] = rope(mm(0, w), tab[:, 0:LANES], tab[:, LANES:2 * LANES]).astype(qa_ref.dtype)
    ka_ref[...] = rope(mm(w, 2 * w), tab[:, 2 * LANES:3 * LANES], tab[:, 3 * LANES:4 * LANES]).astype(ka_ref.dtype)
    va_ref[...] = mm(2 * w, 3 * w).astype(va_ref.dtype)
    qb_ref[...] = (mm(3 * w, 4 * w) * (HEAD_DIM ** -0.5)).astype(qb_ref.dtype)
    kb_ref[...] = mm(4 * w, 5 * w).astype(kb_ref.dtype)
    vb_ref[...] = mm(5 * w, 6 * w).astype(vb_ref.dtype)
    gb_ref[...] = mm(6 * w, 7 * w).astype(gb_ref.dtype)
    u_ref[...] = (mm(7 * w, 8 * w) * mm(8 * w, 9 * w)).astype(u_ref.dtype)
    gate_ref[...] = jax.nn.sigmoid(mm(9 * w, 9 * w + d_gate)).astype(gate_ref.dtype)


def _input_proj(x2d, w_in, rope_tab, seq, tm):
    t, d = x2d.shape
    n_cols = w_in.shape[1]
    d_gate = n_cols - 9 * SC_W
    s_blocks = seq // tm
    row = lambda i: (i, 0)
    small = jax.ShapeDtypeStruct((t, SC_W), ACT_DTYPE)
    return pl.pallas_call(
        functools.partial(_proj_kernel, d_gate=d_gate),
        out_shape=[small] * 8 + [jax.ShapeDtypeStruct((t, d_gate), ACT_DTYPE)],
        grid=(t // tm,),
        in_specs=[pl.BlockSpec((tm, d), row),
                  _resident((d, n_cols), lambda i: (0, 0)),
                  pl.BlockSpec((tm, 4 * LANES), lambda i: (i % s_blocks, 0))],
        out_specs=[pl.BlockSpec((tm, SC_W), row)] * 8 + [pl.BlockSpec((tm, d_gate), row)],
        compiler_params=_params("parallel"),
        name="input_proj",
    )(x2d, w_in, rope_tab)


def _rope_table(seq):
    half = HEAD_DIM // 2
    inv = ROPE_THETA ** (-jnp.arange(half, dtype=jnp.float32) * 2.0 / HEAD_DIM)
    ang = jnp.arange(seq, dtype=jnp.float32)[:, None] * inv[None, :]
    cos, sin = jnp.cos(ang), jnp.sin(ang)
    cos_l = jnp.tile(jnp.concatenate([cos, cos], axis=1), (1, LANES // HEAD_DIM))
    sin_l = jnp.tile(jnp.concatenate([-sin, sin], axis=1), (1, LANES // HEAD_DIM))
    scale = HEAD_DIM ** -0.5
    return jnp.concatenate([cos_l * scale, sin_l * scale, cos_l, sin_l], axis=1)


def _diff_attn_kernel(q_ref, k_ref, v_ref, lq1_ref, lk1_ref, lq2_ref, lk2_ref, g_ref, o_ref,
                      qs_ref, m_ref, acc_ref, *, tq, tk, n_kv, lam_init):
    lane = lax.broadcasted_iota(jnp.int32, (tq, LANES), 1)
    q = q_ref[...]
    zero = jnp.zeros_like(q)
    qs_ref[0:tq, :] = jnp.where(lane < HEAD_DIM, q, zero)
    qs_ref[tq:2 * tq, :] = jnp.where(lane >= HEAD_DIM, q, zero)
    m_ref[...] = jnp.full(m_ref.shape, NEG_BIG, jnp.float32)
    acc_ref[...] = jnp.zeros(acc_ref.shape, jnp.float32)
    ones_col = jnp.where(lax.broadcasted_iota(jnp.int32, (tk, LANES), 1) == 0, 1.0, 0.0).astype(v_ref.dtype)

    def body(j, carry):
        off = pl.multiple_of(j * tk, tk)
        k = k_ref[pl.ds(off, tk), :]
        v = v_ref[pl.ds(off, tk), :]
        s = _dot_nt(qs_ref[...], k)
        m_prev = m_ref[...]
        m_new = jnp.maximum(m_prev, jnp.max(s, axis=1, keepdims=True))
        p = jnp.exp(s - jnp.concatenate([m_new] * (tk // LANES), axis=1))
        alpha = jnp.exp(m_prev - m_new)
        v_ext = jnp.concatenate([v, ones_col], axis=1)
        pv = _dot(p, v_ext)
        acc_ref[...] = acc_ref[...] * jnp.concatenate([alpha, alpha], axis=1) + pv
        m_ref[...] = m_new
        return carry

    lax.fori_loop(0, n_kv, body, 0)

    acc = acc_ref[...]
    o = acc[:, 0:LANES] / acc[:, LANES:LANES + 1]
    lam = (jnp.exp(jnp.sum(lq1_ref[...] * lk1_ref[...], axis=1, keepdims=True))
           - jnp.exp(jnp.sum(lq2_ref[...] * lk2_ref[...], axis=1, keepdims=True)) + lam_init)
    od = o[0:tq] - lam * o[tq:2 * tq]
    ms = jnp.mean(od * od, axis=-1, keepdims=True)
    y = od * lax.rsqrt(ms + LN_EPS) * g_ref[...] * (1.0 - lam_init)
    o_ref[...] = y.astype(o_ref.dtype)


def _diff_attention(qa, ka, va, lq1, lk1, lq2, lk2, subln_g, lam_init, batch, seq, tq, tk):
    t = qa.shape[0]
    n_q = seq // tq
    vec = lambda a: a.reshape(1, -1)
    vspec = lambda n: pl.BlockSpec((1, n), lambda b, h, i: (0, 0))
    return pl.pallas_call(
        functools.partial(_diff_attn_kernel, tq=tq, tk=tk, n_kv=seq // tk, lam_init=lam_init),
        out_shape=jax.ShapeDtypeStruct((t, DA_HEADS * DA_VDIM), ACT_DTYPE),
        grid=(batch, DA_HEADS, n_q),
        in_specs=[pl.BlockSpec((tq, LANES), lambda b, h, i: (b * n_q + i, h)),
                  pl.BlockSpec((seq, LANES), lambda b, h, i: (b, h)),
                  pl.BlockSpec((seq, LANES), lambda b, h, i: (b, h)),
                  vspec(HEAD_DIM), vspec(HEAD_DIM), vspec(HEAD_DIM), vspec(HEAD_DIM), vspec(DA_VDIM)],
        out_specs=pl.BlockSpec((tq, LANES), lambda b, h, i: (b * n_q + i, h)),
        scratch_shapes=[pltpu.VMEM((2 * tq, LANES), qa.dtype),
                        pltpu.VMEM((2 * tq, LANES), jnp.float32),
                        pltpu.VMEM((2 * tq, 2 * LANES), jnp.float32)],
        compiler_params=_params("parallel", "parallel", "parallel"),
        name="diff_attention",
    )(qa, ka, va, vec(lq1), vec(lk1), vec(lq2), vec(lk2), vec(subln_g))


def _na_bias_table(rpb):
    nh, n_dr, n_dc = rpb.shape
    qc = np.arange(GRID_W)
    cs = np.clip(qc - NA_WIN_COLS // 2, 0, GRID_W - NA_WIN_COLS)
    kc = np.arange(GRID_W)
    valid = (kc[None, :] >= cs[:, None]) & (kc[None, :] < cs[:, None] + NA_WIN_COLS)
    p = GRID_W + NA_WIN_COLS
    row = jnp.pad(rpb.astype(jnp.float32), ((0, 0), (0, 0), (0, p - n_dc)))
    toe = jnp.tile(row, (1, 1, GRID_W))[:, :, :GRID_W * (p - 1)].reshape(nh, n_dr, GRID_W, p - 1)
    toe = toe[:, :, :, NA_WIN_COLS - 1:NA_WIN_COLS - 1 + GRID_W]
    toe = jnp.where(jnp.asarray(valid)[None, None], toe, NEG_BIG)
    last = NA_WIN_ROWS - 1
    tab = jnp.stack([toe[:, last - k:last - k + NA_WIN_ROWS] for k in range(NA_WIN_ROWS)], axis=1)
    tab = tab.transpose(0, 1, 3, 2, 4)
    return tab.reshape(nh, NA_WIN_ROWS, GRID_W, NA_WIN_ROWS * GRID_W)


def _na_kernel(q_ref, k_ref, v_ref, bias_ref, o_ref, *, rows, rows_per_step):
    i = pl.program_id(2)
    win = NA_WIN_ROWS * GRID_W
    lane = lax.broadcasted_iota(jnp.int32, (GRID_W, LANES), 1)
    lo = lane < HEAD_DIM
    for rr in range(rows_per_step):
        r = i * rows_per_step + rr
        rs = jnp.clip(r - NA_WIN_ROWS // 2, 0, rows - NA_WIN_ROWS)
        kind = jnp.where(r < NA_WIN_ROWS // 2, r,
                         jnp.where(r > rows - NA_WIN_ROWS // 2, r - (rows - NA_WIN_ROWS), NA_WIN_ROWS // 2))
        off = pl.multiple_of(rs * GRID_W, GRID_W)
        kwin = k_ref[pl.ds(off, win), :]
        vwin = v_ref[pl.ds(off, win), :]
        q = q_ref[rr * GRID_W:(rr + 1) * GRID_W, :]
        zero = jnp.zeros_like(q)
        qst = jnp.concatenate([jnp.where(lo, q, zero), jnp.where(lo, zero, q)], axis=0)
        bias = jnp.concatenate([bias_ref[0, kind], bias_ref[1, kind]], axis=0)
        s = _dot_nt(qst, kwin) + bias
        m = jnp.max(s, axis=1, keepdims=True)
        p = jnp.exp(s - m)
        l = jnp.sum(p, axis=1, keepdims=True)
        pv = _dot(p, vwin) / l
        o = jnp.where(lo, pv[0:GRID_W], pv[GRID_W:2 * GRID_W])
        o_ref[rr * GRID_W:(rr + 1) * GRID_W, :] = o.astype(o_ref.dtype)


def _neighbourhood_attention(qb, kb, vb, bias_tab, batch, seq, rows_per_step):
    t = qb.shape[0]
    rows = seq // GRID_W
    tq = rows_per_step * GRID_W
    n_q = seq // tq
    pairs = NA_HEADS * HEAD_DIM // LANES
    return pl.pallas_call(
        functools.partial(_na_kernel, rows=rows, rows_per_step=rows_per_step),
        out_shape=jax.ShapeDtypeStruct((t, NA_HEADS * HEAD_DIM), ACT_DTYPE),
        grid=(batch, pairs, n_q),
        in_specs=[pl.BlockSpec((tq, LANES), lambda b, h, i: (b * n_q + i, h)),
                  pl.BlockSpec((seq, LANES), lambda b, h, i: (b, h)),
                  pl.BlockSpec((seq, LANES), lambda b, h, i: (b, h)),
                  pl.BlockSpec((2, NA_WIN_ROWS, GRID_W, NA_WIN_ROWS * GRID_W), lambda b, h, i: (h, 0, 0, 0))],
        out_specs=pl.BlockSpec((tq, LANES), lambda b, h, i: (b * n_q + i, h)),
        compiler_params=_params("parallel", "parallel", "parallel"),
        name="neighbourhood_attention",
    )(qb, kb, vb, bias_tab)


def _shift_rows(a, prev_row, next_row):
    n = a.shape[0]
    ridx = lax.broadcasted_iota(jnp.int32, a.shape, 0)
    up = jnp.where(ridx == 0, prev_row, pltpu.roll(a, 1, 0))
    dn = jnp.where(ridx == n - 1, next_row, pltpu.roll(a, n - 1, 0))
    return up, dn


def _merge_kernel(x_ref, ya_ref, yb_ref, gb_ref, u_ref, up_ref, un_ref, gate_ref, cw_ref, wb_ref, wm_ref,
                  g_ref, b_ref, o_ref, *, s_blocks, halo, alpha, d_model):
    i = pl.program_id(0)
    has_prev = i % s_blocks != 0
    has_next = i % s_blocks != s_blocks - 1
    u = u_ref[...].astype(jnp.float32)
    prev_row = jnp.where(has_prev, up_ref[halo - 1:halo, :].astype(jnp.float32), 0.0)
    next_row = jnp.where(has_next, un_ref[0:1, :].astype(jnp.float32), 0.0)
    u_up, u_dn = _shift_rows(u, prev_row, next_row)
    cw = cw_ref[...]
    conv = u_up * cw[0:1, :] + u * cw[1:2, :] + u_dn * cw[2:3, :]
    yc = gb_ref[...].astype(jnp.float32) * conv
    gate = gate_ref[...].astype(jnp.float32)
    merged = (gate[:, 0:d_model] * _dot(ya_ref[...], wb_ref[0])
              + gate[:, d_model:2 * d_model] * _dot(yb_ref[...], wb_ref[1])
              + gate[:, 2 * d_model:3 * d_model] * _dot(yc, wb_ref[2]))
    h = _dot(merged, wm_ref[...])
    o_ref[...] = _layer_norm(alpha * x_ref[...] + h, g_ref[...], b_ref[...])


def _merge(x2d, ya, yb, gb, u, gates, conv_w, w_branch, w_mix, ln_g, ln_b, seq, tm, alpha):
    t, d = x2d.shape
    halo = 16
    hb = tm // halo
    n_halo = t // halo
    row = lambda i: (i, 0)
    const2 = lambda i: (0, 0)
    return pl.pallas_call(
        functools.partial(_merge_kernel, s_blocks=seq // tm, halo=halo, alpha=alpha, d_model=d),
        out_shape=jax.ShapeDtypeStruct((t, d), jnp.float32),
        grid=(t // tm,),
        in_specs=[pl.BlockSpec((tm, d), row),
                  pl.BlockSpec((tm, SC_W), row),
                  pl.BlockSpec((tm, SC_W), row),
                  pl.BlockSpec((tm, SC_W), row),
                  pl.BlockSpec((tm, SC_W), row),
                  pl.BlockSpec((halo, SC_W), lambda i: (jnp.maximum(i * hb - 1, 0), 0)),
                  pl.BlockSpec((halo, SC_W), lambda i: (jnp.minimum((i + 1) * hb, n_halo - 1), 0)),
                  pl.BlockSpec((tm, N_BRANCH * d), row),
                  pl.BlockSpec(conv_w.shape, const2),
                  _resident(w_branch.shape, lambda i: (0, 0, 0)),
                  _resident(w_mix.shape, const2),
                  pl.BlockSpec((1, d), const2),
                  pl.BlockSpec((1, d), const2)],
        out_specs=pl.BlockSpec((tm, d), row),
        compiler_params=_params("parallel"),
        name="branch_merge",
    )(x2d, ya, yb, gb, u, u, u, gates, conv_w, w_branch, w_mix, ln_g.reshape(1, d), ln_b.reshape(1, d))


def _kv_kernel(m_ref, w_ref, o_ref):
    o_ref[...] = _dot(m_ref[...], w_ref[...]).astype(o_ref.dtype)


def _mem_kv(mem2d, xa_kv):
    t, d = mem2d.shape
    n = xa_kv.shape[1]
    tm = min(t, 256)
    return pl.pallas_call(
        _kv_kernel,
        out_shape=jax.ShapeDtypeStruct((t, n), ACT_DTYPE),
        grid=(t // tm,),
        in_specs=[pl.BlockSpec((tm, d), lambda i: (i, 0)), _resident((d, n), lambda i: (0, 0))],
        out_specs=pl.BlockSpec((tm, n), lambda i: (i, 0)),
        compiler_params=_params("parallel"),
        name="mem_kv",
    )(mem2d, xa_kv)


def _xattn_kernel(x_ref, k_ref, v_ref, wq_ref, wo_ref, g_ref, b_ref, o_ref, *, alpha):
    x = x_ref[...]
    d = x.shape[1]
    hd = d // XA_HEADS
    q = _dot(x, wq_ref[...]) * (hd ** -0.5)
    k = k_ref[...]
    v = v_ref[...]
    outs = []
    for h in range(XA_HEADS):
        sl = slice(h * hd, (h + 1) * hd)
        s = _dot_nt(q[:, sl], k[:, sl])
        m = jnp.max(s, axis=1, keepdims=True)
        p = jnp.exp(s - m)
        l = jnp.sum(p, axis=1, keepdims=True)
        outs.append(_dot(p, v[:, sl]) / l)
    o = jnp.concatenate(outs, axis=1)
    h_out = _dot(o, wo_ref[...])
    o_ref[...] = _layer_norm(alpha * x + h_out, g_ref[...], b_ref[...])


def _cross_attention(x2d, kv, xa_q, xa_o, ln_g, ln_b, seq, mem_len, tm, alpha):
    t, d = x2d.shape
    s_blocks = seq // tm
    row = lambda i: (i, 0)
    const2 = lambda i: (0, 0)
    return pl.pallas_call(
        functools.partial(_xattn_kernel, alpha=alpha),
        out_shape=jax.ShapeDtypeStruct((t, d), jnp.float32),
        grid=(t // tm,),
        in_specs=[pl.BlockSpec((tm, d), row),
                  pl.BlockSpec((mem_len, d), lambda i: (i // s_blocks, 0)),
                  pl.BlockSpec((mem_len, d), lambda i: (i // s_blocks, 1)),
                  _resident((d, d), const2),
                  _resident((d, d), const2),
                  pl.BlockSpec((1, d), const2),
                  pl.BlockSpec((1, d), const2)],
        out_specs=pl.BlockSpec((tm, d), row),
        compiler_params=_params("parallel"),
        name="cross_attention",
    )(x2d, kv, kv, xa_q, xa_o, ln_g.reshape(1, d), ln_b.reshape(1, d))


def _ffn_kernel(x_ref, xp_ref, xn_ref, wi_ref, cw_ref, cb_ref, wo_ref, g_ref, b_ref, o_ref, acc_ref,
                *, s_blocks, halo, alpha, d_ff, tf):
    i = pl.program_id(0)
    has_prev = i % s_blocks != 0
    has_next = i % s_blocks != s_blocks - 1
    x = x_ref[...]
    xb = x.astype(MXU_DTYPE)
    xe = jnp.concatenate([jnp.where(has_prev, xp_ref[halo - 1:halo, :], 0.0),
                          jnp.where(has_next, xn_ref[0:1, :], 0.0),
                          jnp.zeros((halo - 2, x.shape[1]), jnp.float32)], axis=0).astype(MXU_DTYPE)
    for c in range(d_ff // tf):
        w_u = wi_ref[:, c * tf:(c + 1) * tf]
        w_g = wi_ref[:, d_ff + c * tf:d_ff + (c + 1) * tf]
        u = jnp.dot(xb, w_u, preferred_element_type=jnp.float32)
        gt = jnp.dot(xb, w_g, preferred_element_type=jnp.float32)
        ge = jnp.dot(xe, w_g, preferred_element_type=jnp.float32)
        g_up, g_dn = _shift_rows(gt, ge[0:1, :], ge[1:2, :])
        cw = cw_ref[:, c * tf:(c + 1) * tf]
        a = g_up * cw[0:1, :] + gt * cw[1:2, :] + g_dn * cw[2:3, :] + cb_ref[:, c * tf:(c + 1) * tf]
        hidden = (a * jax.nn.sigmoid(a)) * u
        part = _dot(hidden, wo_ref[c * tf:(c + 1) * tf, :])
        if c == 0:
            acc_ref[...] = part
        else:
            acc_ref[...] += part
    o_ref[...] = _layer_norm(alpha * x + acc_ref[...], g_ref[...], b_ref[...])


def _conv_ffn(x2d, w_in, conv_w, conv_b, w_out, ln_g, ln_b, seq, tm, alpha):
    t, d = x2d.shape
    d_ff = w_out.shape[0]
    tf = 256
    halo = 8
    hb = tm // halo
    n_halo = t // halo
    row = lambda i: (i, 0)
    const2 = lambda i: (0, 0)
    return pl.pallas_call(
        functools.partial(_ffn_kernel, s_blocks=seq // tm, halo=halo, alpha=alpha, d_ff=d_ff, tf=tf),
        out_shape=jax.ShapeDtypeStruct((t, d), jnp.float32),
        grid=(t // tm,),
        in_specs=[pl.BlockSpec((tm, d), row),
                  pl.BlockSpec((halo, d), lambda i: (jnp.maximum(i * hb - 1, 0), 0)),
                  pl.BlockSpec((halo, d), lambda i: (jnp.minimum((i + 1) * hb, n_halo - 1), 0)),
                  _resident(w_in.shape, const2),
                  pl.BlockSpec(conv_w.shape, const2),
                  pl.BlockSpec((1, d_ff), const2),
                  _resident(w_out.shape, const2),
                  pl.BlockSpec((1, d), const2),
                  pl.BlockSpec((1, d), const2)],
        out_specs=pl.BlockSpec((tm, d), row),
        scratch_shapes=[pltpu.VMEM((tm, d), jnp.float32)],
        compiler_params=_params("parallel"),
        name="conv_ffn",
    )(x2d, x2d, x2d, w_in, conv_w, conv_b.reshape(1, d_ff), w_out, ln_g.reshape(1, d), ln_b.reshape(1, d))


def kernel(x, mem, emb_ln_g, emb_ln_b, w_in, lam_q1, lam_k1, lam_q2, lam_k2, subln_g, rpb, sc_conv_w, w_branch,
           w_mix_out, xa_q, xa_kv, xa_o, ffn_w_in, ffn_conv_w, ffn_conv_b, ffn_w_out, ln_g, ln_b):
    batch, seq, d = x.shape
    depth = w_in.shape[0]
    mem_len = mem.shape[1]
    assert seq % GRID_W == 0 and seq // GRID_W >= NA_WIN_ROWS
    alpha = (2.0 * depth) ** 0.25
    tm = min(512, seq)
    tq = min(512, seq)
    tk = min(1024, seq)
    wcast = lambda a: a.astype(MXU_DTYPE)

    rope_tab = _rope_table(seq)
    h = _embed_ln(x.reshape(batch * seq, d), emb_ln_g, emb_ln_b, tm)
    mem2d = mem.reshape(batch * mem_len, d)
    for l in range(depth):
        lam_init = 0.8 - 0.6 * math.exp(-0.3 * l)
        qa, ka, va, qb, kb, vb, gb, u, gates = _input_proj(h, wcast(w_in[l]), rope_tab, seq, tm)
        ya = _diff_attention(qa, ka, va, lam_q1[l], lam_k1[l], lam_q2[l], lam_k2[l], subln_g[l], lam_init,
                             batch, seq, tq, tk)
        yb = _neighbourhood_attention(qb, kb, vb, _na_bias_table(rpb[l]), batch, seq, NA_WIN_ROWS)
        h = _merge(h, ya, yb, gb, u, gates, sc_conv_w[l], wcast(w_branch[l]), wcast(w_mix_out[l]),
                   ln_g[l, 0], ln_b[l, 0], seq, tm, alpha)
        kv = _mem_kv(mem2d, wcast(xa_kv[l]))
        h = _cross_attention(h, kv, wcast(xa_q[l]), wcast(xa_o[l]), ln_g[l, 1], ln_b[l, 1], seq, mem_len, tm, alpha)
        h = _conv_ffn(h, wcast(ffn_w_in[l]), ffn_conv_w[l], ffn_conv_b[l], wcast(ffn_w_out[l]),
                      ln_g[l, 2], ln_b[l, 2], seq, tm, alpha)
    return h.reshape(batch, seq, d)
```

```python
import functools
import math

import numpy as np
import jax
import jax.numpy as jnp
from jax import lax
from jax.experimental import pallas as pl
from jax.experimental.pallas import tpu as pltpu

GRID_W = 64
HEAD_DIM = 64
DA_HEADS = 4
DA_VDIM = 2 * HEAD_DIM
NA_HEADS = 8
NA_WIN_ROWS = 8
NA_WIN_COLS = 16
SC_W = 512
N_BRANCH = 3
XA_HEADS = 4
ROPE_THETA = 10000.0
LN_EPS = 1e-5

LANES = 128
MXU_DTYPE = jnp.bfloat16
ACT_DTYPE = jnp.bfloat16
NEG_BIG = -1e30
VMEM_LIMIT = 56 * 1024 * 1024
FFN_CHUNK = 256


def _params(*sem):
    return pltpu.CompilerParams(dimension_semantics=sem, vmem_limit_bytes=VMEM_LIMIT)


def _resident(shape, index_map):
    return pl.BlockSpec(shape, index_map, pipeline_mode=pl.Buffered(1))


def _layer_norm(v, g, b):
    mu = jnp.mean(v, axis=-1, keepdims=True)
    d = v - mu
    var = jnp.mean(d * d, axis=-1, keepdims=True)
    return d * lax.rsqrt(var + LN_EPS) * g + b


def _dot(a, b):
    return jnp.dot(a.astype(MXU_DTYPE), b.astype(MXU_DTYPE), preferred_element_type=jnp.float32)


def _dot_nt(a, b):
    return lax.dot_general(a.astype(MXU_DTYPE), b.astype(MXU_DTYPE), (((1,), (1,)), ((), ())),
                           preferred_element_type=jnp.float32)


def _ln_kernel(x_ref, g_ref, b_ref, o_ref):
    o_ref[...] = _layer_norm(x_ref[...], g_ref[...], b_ref[...])


def _embed_ln(x2d, g, b, tm):
    t, d = x2d.shape
    return pl.pallas_call(
        _ln_kernel,
        out_shape=jax.ShapeDtypeStruct((t, d), jnp.float32),
        grid=(t // tm,),
        in_specs=[pl.BlockSpec((tm, d), lambda i: (i, 0)),
                  pl.BlockSpec((1, d), lambda i: (0, 0)),
                  pl.BlockSpec((1, d), lambda i: (0, 0))],
        out_specs=pl.BlockSpec((tm, d), lambda i: (i, 0)),
        compiler_params=_params("parallel"),
        name="embed_ln",
    )(x2d, g.reshape(1, d), b.reshape(1, d))


def _proj_kernel(x_ref, w_ref, rope_ref, qa_ref, ka_ref, va_ref, qb_ref, kb_ref, vb_ref,
                 gb_ref, u_ref, gate_ref, *, d_gate):
    xb = x_ref[...].astype(MXU_DTYPE)
    tm = xb.shape[0]
    w = SC_W

    def mm(c0, c1):
        return jnp.dot(xb, w_ref[:, c0:c1], preferred_element_type=jnp.float32)

    lane = lax.broadcasted_iota(jnp.int32, (tm, LANES), 1)
    first_half = (lane % HEAD_DIM) < (HEAD_DIM // 2)

    def rope(y, cos, sin):
        outs = []
        for c in range(y.shape[1] // LANES):
            yc = y[:, c * LANES:(c + 1) * LANES]
            rot = jnp.where(first_half, pltpu.roll(yc, LANES - HEAD_DIM // 2, 1),
                            pltpu.roll(yc, HEAD_DIM // 2, 1))
            outs.append(yc * cos + rot * sin)
        return jnp.concatenate(outs, axis=1)

    tab = rope_ref[...]
    qa_ref[...] = rope(mm(0, w), tab[:, 0:LANES], tab[:, LANES:2 * LANES]).astype(qa_ref.dtype)
    ka_ref[...] = rope(mm(w, 2 * w), tab[:, 2 * LANES:3 * LANES], tab[:, 3 * LANES:4 * LANES]).astype(ka_ref.dtype)
    va_ref[...] = mm(2 * w, 3 * w).astype(va_ref.dtype)
    qb_ref[...] = (mm(3 * w, 4 * w) * (HEAD_DIM ** -0.5)).astype(qb_ref.dtype)
    kb_ref[...] = mm(4 * w, 5 * w).astype(kb_ref.dtype)
    vb_ref[...] = mm(5 * w, 6 * w).astype(vb_ref.dtype)
    gb_ref[...] = mm(6 * w, 7 * w).astype(gb_ref.dtype)
    u_ref[...] = (mm(7 * w, 8 * w) * mm(8 * w, 9 * w)).astype(u_ref.dtype)
    gate_ref[...] = jax.nn.sigmoid(mm(9 * w, 9 * w + d_gate)).astype(gate_ref.dtype)


def _input_proj(x2d, w_in, rope_tab, seq, tm):
    t, d = x2d.shape
    n_cols = w_in.shape[1]
    d_gate = n_cols - 9 * SC_W
    s_blocks = seq // tm
    row = lambda i: (i, 0)
    small = jax.ShapeDtypeStruct((t, SC_W), ACT_DTYPE)
    return pl.pallas_call(
        functools.partial(_proj_kernel, d_gate=d_gate),
        out_shape=[small] * 8 + [jax.ShapeDtypeStruct((t, d_gate), ACT_DTYPE)],
        grid=(t // tm,),
        in_specs=[pl.BlockSpec((tm, d), row),
                  _resident((d, n_cols), lambda i: (0, 0)),
                  pl.BlockSpec((tm, 4 * LANES), lambda i: (i % s_blocks, 0))],
        out_specs=[pl.BlockSpec((tm, SC_W), row)] * 8 + [pl.BlockSpec((tm, d_gate), row)],
        compiler_params=_params("parallel"),
        name="input_proj",
    )(x2d, w_in, rope_tab)


def _rope_table(seq):
    half = HEAD_DIM // 2
    inv = ROPE_THETA ** (-jnp.arange(half, dtype=jnp.float32) * 2.0 / HEAD_DIM)
    ang = jnp.arange(seq, dtype=jnp.float32)[:, None] * inv[None, :]
    cos, sin = jnp.cos(ang), jnp.sin(ang)
    cos_l = jnp.tile(jnp.concatenate([cos, cos], axis=1), (1, LANES // HEAD_DIM))
    sin_l = jnp.tile(jnp.concatenate([-sin, sin], axis=1), (1, LANES // HEAD_DIM))
    scale = HEAD_DIM ** -0.5
    return jnp.concatenate([cos_l * scale, sin_l * scale, cos_l, sin_l], axis=1)


def _diff_attn_kernel(q_ref, k_ref, v_ref, lq1_ref, lk1_ref, lq2_ref, lk2_ref, g_ref, o_ref,
                      qs_ref, s0_ref, s1_ref, m_ref, acc_ref, *, tq, tk, rc, n_kv, lam_init):
    lane = lax.broadcasted_iota(jnp.int32, (tq, LANES), 1)
    q = q_ref[...]
    zero = jnp.zeros_like(q)
    qs_ref[0:tq, :] = jnp.where(lane < HEAD_DIM, q, zero)
    qs_ref[tq:2 * tq, :] = jnp.where(lane >= HEAD_DIM, q, zero)
    m_ref[...] = jnp.full(m_ref.shape, NEG_BIG, jnp.float32)
    acc_ref[...] = jnp.zeros(acc_ref.shape, jnp.float32)
    ones_col = jnp.where(lax.broadcasted_iota(jnp.int32, (tk, LANES), 1) == 0, 1.0, 0.0).astype(v_ref.dtype)

    def scores(j, s_ref):
        off = pl.multiple_of(j * tk, tk)
        s_ref[...] = _dot_nt(qs_ref[...], k_ref[pl.ds(off, tk), :])

    def update(j, s_ref):
        off = pl.multiple_of(j * tk, tk)
        v_ext = jnp.concatenate([v_ref[pl.ds(off, tk), :], ones_col], axis=1)
        for c in range(2 * tq // rc):
            rows = slice(c * rc, (c + 1) * rc)
            s = s_ref[rows, :]
            m_prev = m_ref[rows, :]
            m_new = jnp.maximum(m_prev, jnp.max(s, axis=1, keepdims=True))
            p = jnp.exp(s - jnp.concatenate([m_new] * (tk // LANES), axis=1))
            alpha = jnp.exp(m_prev - m_new)
            pv = _dot(p, v_ext)
            acc_ref[rows, :] = acc_ref[rows, :] * jnp.concatenate([alpha, alpha], axis=1) + pv
            m_ref[rows, :] = m_new

    scores(0, s0_ref)

    def body(i, carry):
        scores(2 * i + 1, s1_ref)
        update(2 * i, s0_ref)
        scores(2 * i + 2, s0_ref)
        update(2 * i + 1, s1_ref)
        return carry

    lax.fori_loop(0, n_kv // 2 - 1, body, 0)
    scores(n_kv - 1, s1_ref)
    update(n_kv - 2, s0_ref)
    update(n_kv - 1, s1_ref)

    acc = acc_ref[...]
    o = acc[:, 0:LANES] / acc[:, LANES:LANES + 1]
    lam = (jnp.exp(jnp.sum(lq1_ref[...] * lk1_ref[...], axis=1, keepdims=True))
           - jnp.exp(jnp.sum(lq2_ref[...] * lk2_ref[...], axis=1, keepdims=True)) + lam_init)
    od = o[0:tq] - lam * o[tq:2 * tq]
    ms = jnp.mean(od * od, axis=-1, keepdims=True)
    y = od * lax.rsqrt(ms + LN_EPS) * g_ref[...] * (1.0 - lam_init)
    o_ref[...] = y.astype(o_ref.dtype)


def _diff_attention(qa, ka, va, lq1, lk1, lq2, lk2, subln_g, lam_init, batch, seq, tq, tk):
    t = qa.shape[0]
    n_q = seq // tq
    n_kv = seq // tk
    rc = 256
    assert n_kv % 2 == 0 and (2 * tq) % rc == 0
    vec = lambda a: a.reshape(1, -1)
    vspec = lambda n: pl.BlockSpec((1, n), lambda b, h, i: (0, 0))
    return pl.pallas_call(
        functools.partial(_diff_attn_kernel, tq=tq, tk=tk, rc=rc, n_kv=n_kv, lam_init=lam_init),
        out_shape=jax.ShapeDtypeStruct((t, DA_HEADS * DA_VDIM), ACT_DTYPE),
        grid=(batch, DA_HEADS, n_q),
        in_specs=[pl.BlockSpec((tq, LANES), lambda b, h, i: (b * n_q + i, h)),
                  pl.BlockSpec((seq, LANES), lambda b, h, i: (b, h)),
                  pl.BlockSpec((seq, LANES), lambda b, h, i: (b, h)),
                  vspec(HEAD_DIM), vspec(HEAD_DIM), vspec(HEAD_DIM), vspec(HEAD_DIM), vspec(DA_VDIM)],
        out_specs=pl.BlockSpec((tq, LANES), lambda b, h, i: (b * n_q + i, h)),
        scratch_shapes=[pltpu.VMEM((2 * tq, LANES), qa.dtype),
                        pltpu.VMEM((2 * tq, tk), jnp.float32),
                        pltpu.VMEM((2 * tq, tk), jnp.float32),
                        pltpu.VMEM((2 * tq, LANES), jnp.float32),
                        pltpu.VMEM((2 * tq, 2 * LANES), jnp.float32)],
        compiler_params=_params("parallel", "parallel", "parallel"),
        name="diff_attention",
    )(qa, ka, va, vec(lq1), vec(lk1), vec(lq2), vec(lk2), vec(subln_g))


def _na_bias_table(rpb):
    nh, n_dr, n_dc = rpb.shape
    qc = np.arange(GRID_W)
    cs = np.clip(qc - NA_WIN_COLS // 2, 0, GRID_W - NA_WIN_COLS)
    kc = np.arange(GRID_W)
    valid = (kc[None, :] >= cs[:, None]) & (kc[None, :] < cs[:, None] + NA_WIN_COLS)
    p = GRID_W + NA_WIN_COLS
    row = jnp.pad(rpb.astype(jnp.float32), ((0, 0), (0, 0), (0, p - n_dc)))
    toe = jnp.tile(row, (1, 1, GRID_W))[:, :, :GRID_W * (p - 1)].reshape(nh, n_dr, GRID_W, p - 1)
    toe = toe[:, :, :, NA_WIN_COLS - 1:NA_WIN_COLS - 1 + GRID_W]
    toe = jnp.where(jnp.asarray(valid)[None, None], toe, NEG_BIG)
    last = NA_WIN_ROWS - 1
    tab = jnp.stack([toe[:, last - k:last - k + NA_WIN_ROWS] for k in range(NA_WIN_ROWS)], axis=1)
    tab = tab.transpose(0, 1, 3, 2, 4)
    return tab.reshape(nh, NA_WIN_ROWS, GRID_W, NA_WIN_ROWS * GRID_W)


def _na_kernel(q_ref, k_ref, v_ref, bias_ref, o_ref, *, rows, rows_per_step):
    i = pl.program_id(2)
    win = NA_WIN_ROWS * GRID_W
    lane = lax.broadcasted_iota(jnp.int32, (GRID_W, LANES), 1)
    lo = lane < HEAD_DIM
    for rr in range(rows_per_step):
        r = i * rows_per_step + rr
        rs = jnp.clip(r - NA_WIN_ROWS // 2, 0, rows - NA_WIN_ROWS)
        kind = jnp.where(r < NA_WIN_ROWS // 2, r,
                         jnp.where(r > rows - NA_WIN_ROWS // 2, r - (rows - NA_WIN_ROWS), NA_WIN_ROWS // 2))
        off = pl.multiple_of(rs * GRID_W, GRID_W)
        kwin = k_ref[pl.ds(off, win), :]
        vwin = v_ref[pl.ds(off, win), :]
        q = q_ref[rr * GRID_W:(rr + 1) * GRID_W, :]
        zero = jnp.zeros_like(q)
        qst = jnp.concatenate([jnp.where(lo, q, zero), jnp.where(lo, zero, q)], axis=0)
        bias = jnp.concatenate([bias_ref[0, kind], bias_ref[1, kind]], axis=0)
        s = _dot_nt(qst, kwin) + bias
        m = jnp.max(s, axis=1, keepdims=True)
        p = jnp.exp(s - m)
        l = jnp.sum(p, axis=1, keepdims=True)
        pv = _dot(p, vwin) / l
        o = jnp.where(lo, pv[0:GRID_W], pv[GRID_W:2 * GRID_W])
        o_ref[rr * GRID_W:(rr + 1) * GRID_W, :] = o.astype(o_ref.dtype)


def _neighbourhood_attention(qb, kb, vb, bias_tab, batch, seq, rows_per_step):
    t = qb.shape[0]
    rows = seq // GRID_W
    tq = rows_per_step * GRID_W
    n_q = seq // tq
    pairs = NA_HEADS * HEAD_DIM // LANES
    return pl.pallas_call(
        functools.partial(_na_kernel, rows=rows, rows_per_step=rows_per_step),
        out_shape=jax.ShapeDtypeStruct((t, NA_HEADS * HEAD_DIM), ACT_DTYPE),
        grid=(batch, pairs, n_q),
        in_specs=[pl.BlockSpec((tq, LANES), lambda b, h, i: (b * n_q + i, h)),
                  pl.BlockSpec((seq, LANES), lambda b, h, i: (b, h)),
                  pl.BlockSpec((seq, LANES), lambda b, h, i: (b, h)),
                  pl.BlockSpec((2, NA_WIN_ROWS, GRID_W, NA_WIN_ROWS * GRID_W), lambda b, h, i: (h, 0, 0, 0))],
        out_specs=pl.BlockSpec((tq, LANES), lambda b, h, i: (b * n_q + i, h)),
        compiler_params=_params("parallel", "parallel", "parallel"),
        name="neighbourhood_attention",
    )(qb, kb, vb, bias_tab)


def _shift_rows(a, prev_row, next_row):
    n = a.shape[0]
    ridx = lax.broadcasted_iota(jnp.int32, a.shape, 0)
    up = jnp.where(ridx == 0, prev_row, pltpu.roll(a, 1, 0))
    dn = jnp.where(ridx == n - 1, next_row, pltpu.roll(a, n - 1, 0))
    return up, dn


def _merge_kernel(x_ref, ya_ref, yb_ref, gb_ref, u_ref, up_ref, un_ref, gate_ref, cw_ref, wb_ref, wm_ref,
                  g_ref, b_ref, o_ref, *, s_blocks, halo, alpha, d_model):
    i = pl.program_id(0)
    has_prev = i % s_blocks != 0
    has_next = i % s_blocks != s_blocks - 1
    u = u_ref[...].astype(jnp.float32)
    prev_row = jnp.where(has_prev, up_ref[halo - 1:halo, :].astype(jnp.float32), 0.0)
    next_row = jnp.where(has_next, un_ref[0:1, :].astype(jnp.float32), 0.0)
    u_up, u_dn = _shift_rows(u, prev_row, next_row)
    cw = cw_ref[...]
    conv = u_up * cw[0:1, :] + u * cw[1:2, :] + u_dn * cw[2:3, :]
    yc = gb_ref[...].astype(jnp.float32) * conv
    gate = gate_ref[...].astype(jnp.float32)
    merged = (gate[:, 0:d_model] * _dot(ya_ref[...], wb_ref[0])
              + gate[:, d_model:2 * d_model] * _dot(yb_ref[...], wb_ref[1])
              + gate[:, 2 * d_model:3 * d_model] * _dot(yc, wb_ref[2]))
    h = _dot(merged, wm_ref[...])
    o_ref[...] = _layer_norm(alpha * x_ref[...] + h, g_ref[...], b_ref[...])


def _merge(x2d, ya, yb, gb, u, gates, conv_w, w_branch, w_mix, ln_g, ln_b, seq, tm, alpha):
    t, d = x2d.shape
    halo = 16
    hb = tm // halo
    n_halo = t // halo
    row = lambda i: (i, 0)
    const2 = lambda i: (0, 0)
    return pl.pallas_call(
        functools.partial(_merge_kernel, s_blocks=seq // tm, halo=halo, alpha=alpha, d_model=d),
        out_shape=jax.ShapeDtypeStruct((t, d), jnp.float32),
        grid=(t // tm,),
        in_specs=[pl.BlockSpec((tm, d), row),
                  pl.BlockSpec((tm, SC_W), row),
                  pl.BlockSpec((tm, SC_W), row),
                  pl.BlockSpec((tm, SC_W), row),
                  pl.BlockSpec((tm, SC_W), row),
                  pl.BlockSpec((halo, SC_W), lambda i: (jnp.maximum(i * hb - 1, 0), 0)),
                  pl.BlockSpec((halo, SC_W), lambda i: (jnp.minimum((i + 1) * hb, n_halo - 1), 0)),
                  pl.BlockSpec((tm, N_BRANCH * d), row),
                  pl.BlockSpec(conv_w.shape, const2),
                  _resident(w_branch.shape, lambda i: (0, 0, 0)),
                  _resident(w_mix.shape, const2),
                  pl.BlockSpec((1, d), const2),
                  pl.BlockSpec((1, d), const2)],
        out_specs=pl.BlockSpec((tm, d), row),
        compiler_params=_params("parallel"),
        name="branch_merge",
    )(x2d, ya, yb, gb, u, u, u, gates, conv_w, w_branch, w_mix, ln_g.reshape(1, d), ln_b.reshape(1, d))


def _kv_kernel(m_ref, w_ref, o_ref):
    o_ref[...] = _dot(m_ref[...], w_ref[...]).astype(o_ref.dtype)


def _mem_kv(mem2d, xa_kv):
    t, d = mem2d.shape
    n = xa_kv.shape[1]
    tm = min(t, 256)
    return pl.pallas_call(
        _kv_kernel,
        out_shape=jax.ShapeDtypeStruct((t, n), ACT_DTYPE),
        grid=(t // tm,),
        in_specs=[pl.BlockSpec((tm, d), lambda i: (i, 0)), _resident((d, n), lambda i: (0, 0))],
        out_specs=pl.BlockSpec((tm, n), lambda i: (i, 0)),
        compiler_params=_params("parallel"),
        name="mem_kv",
    )(mem2d, xa_kv)


def _xattn_kernel(x_ref, k_ref, v_ref, wq_ref, wo_ref, g_ref, b_ref, o_ref, *, alpha):
    x = x_ref[...]
    d = x.shape[1]
    hd = d // XA_HEADS
    q = _dot(x, wq_ref[...]) * (hd ** -0.5)
    k = k_ref[...]
    v = v_ref[...]
    outs = []
    for h in range(XA_HEADS):
        sl = slice(h * hd, (h + 1) * hd)
        s = _dot_nt(q[:, sl], k[:, sl])
        m = jnp.max(s, axis=1, keepdims=True)
        p = jnp.exp(s - m)
        l = jnp.sum(p, axis=1, keepdims=True)
        outs.append(_dot(p, v[:, sl]) / l)
    o = jnp.concatenate(outs, axis=1)
    h_out = _dot(o, wo_ref[...])
    o_ref[...] = _layer_norm(alpha * x + h_out, g_ref[...], b_ref[...])


def _cross_attention(x2d, kv, xa_q, xa_o, ln_g, ln_b, seq, mem_len, tm, alpha):
    t, d = x2d.shape
    s_blocks = seq // tm
    row = lambda i: (i, 0)
    const2 = lambda i: (0, 0)
    return pl.pallas_call(
        functools.partial(_xattn_kernel, alpha=alpha),
        out_shape=jax.ShapeDtypeStruct((t, d), jnp.float32),
        grid=(t // tm,),
        in_specs=[pl.BlockSpec((tm, d), row),
                  pl.BlockSpec((mem_len, d), lambda i: (i // s_blocks, 0)),
                  pl.BlockSpec((mem_len, d), lambda i: (i // s_blocks, 1)),
                  _resident((d, d), const2),
                  _resident((d, d), const2),
                  pl.BlockSpec((1, d), const2),
                  pl.BlockSpec((1, d), const2)],
        out_specs=pl.BlockSpec((tm, d), row),
        compiler_params=_params("parallel"),
        name="cross_attention",
    )(x2d, kv, kv, xa_q, xa_o, ln_g.reshape(1, d), ln_b.reshape(1, d))


def _ffn_kernel(x_ref, xp_ref, xn_ref, wi_ref, cw_ref, cb_ref, wo_ref, g_ref, b_ref, o_ref, hid_ref,
                *, s_blocks, halo, alpha, d_ff, tf):
    i = pl.program_id(0)
    has_prev = i % s_blocks != 0
    has_next = i % s_blocks != s_blocks - 1
    x = x_ref[...]
    xb = x.astype(MXU_DTYPE)
    xe = jnp.concatenate([jnp.where(has_prev, xp_ref[halo - 1:halo, :], 0.0),
                          jnp.where(has_next, xn_ref[0:1, :], 0.0),
                          jnp.zeros((halo - 2, x.shape[1]), jnp.float32)], axis=0).astype(MXU_DTYPE)
    for c in range(d_ff // tf):
        w_c = wi_ref[:, 2 * c * tf:2 * (c + 1) * tf]
        ug = jnp.dot(xb, w_c, preferred_element_type=jnp.float32)
        u, gt = ug[:, 0:tf], ug[:, tf:2 * tf]
        ge = jnp.dot(xe, w_c, preferred_element_type=jnp.float32)[:, tf:2 * tf]
        g_up, g_dn = _shift_rows(gt, ge[0:1, :], ge[1:2, :])
        cw = cw_ref[:, c * tf:(c + 1) * tf]
        a = g_up * cw[0:1, :] + gt * cw[1:2, :] + g_dn * cw[2:3, :] + cb_ref[:, c * tf:(c + 1) * tf]
        hid_ref[:, c * tf:(c + 1) * tf] = ((a * jax.nn.sigmoid(a)) * u).astype(hid_ref.dtype)
    h = jnp.dot(hid_ref[...], wo_ref[...], preferred_element_type=jnp.float32)
    o_ref[...] = _layer_norm(alpha * x + h, g_ref[...], b_ref[...])


def _conv_ffn(x2d, w_in, conv_w, conv_b, w_out, ln_g, ln_b, seq, tm, alpha):
    t, d = x2d.shape
    d_ff = w_out.shape[0]
    tf = FFN_CHUNK
    halo = 8
    hb = tm // halo
    n_halo = t // halo
    row = lambda i: (i, 0)
    const2 = lambda i: (0, 0)
    return pl.pallas_call(
        functools.partial(_ffn_kernel, s_blocks=seq // tm, halo=halo, alpha=alpha, d_ff=d_ff, tf=tf),
        out_shape=jax.ShapeDtypeStruct((t, d), jnp.float32),
        grid=(t // tm,),
        in_specs=[pl.BlockSpec((tm, d), row),
                  pl.BlockSpec((halo, d), lambda i: (jnp.maximum(i * hb - 1, 0), 0)),
                  pl.BlockSpec((halo, d), lambda i: (jnp.minimum((i + 1) * hb, n_halo - 1), 0)),
                  _resident(w_in.shape, const2),
                  pl.BlockSpec(conv_w.shape, const2),
                  pl.BlockSpec((1, d_ff), const2),
                  _resident(w_out.shape, const2),
                  pl.BlockSpec((1, d), const2),
                  pl.BlockSpec((1, d), const2)],
        out_specs=pl.BlockSpec((tm, d), row),
        scratch_shapes=[pltpu.VMEM((tm, d_ff), MXU_DTYPE)],
        compiler_params=_params("parallel"),
        name="conv_ffn",
    )(x2d, x2d, x2d, w_in, conv_w, conv_b.reshape(1, d_ff), w_out, ln_g.reshape(1, d), ln_b.reshape(1, d))


def _ffn_group_columns(w_in, tf):
    d, two_ff = w_in.shape
    n = two_ff // (2 * tf)
    return w_in.reshape(d, 2, n, tf).transpose(0, 2, 1, 3).reshape(d, two_ff)


def kernel(x, mem, emb_ln_g, emb_ln_b, w_in, lam_q1, lam_k1, lam_q2, lam_k2, subln_g, rpb, sc_conv_w, w_branch,
           w_mix_out, xa_q, xa_kv, xa_o, ffn_w_in, ffn_conv_w, ffn_conv_b, ffn_w_out, ln_g, ln_b):
    batch, seq, d = x.shape
    depth = w_in.shape[0]
    mem_len = mem.shape[1]
    assert seq % GRID_W == 0 and seq // GRID_W >= NA_WIN_ROWS
    alpha = (2.0 * depth) ** 0.25
    tm = min(512, seq)
    tq = min(512, seq)
    tk = min(1024, seq // 2)
    wcast = lambda a: a.astype(MXU_DTYPE)

    rope_tab = _rope_table(seq)
    h = _embed_ln(x.reshape(batch * seq, d), emb_ln_g, emb_ln_b, tm)
    mem2d = mem.reshape(batch * mem_len, d)
    for l in range(depth):
        lam_init = 0.8 - 0.6 * math.exp(-0.3 * l)
        qa, ka, va, qb, kb, vb, gb, u, gates = _input_proj(h, wcast(w_in[l]), rope_tab, seq, tm)
        ya = _diff_attention(qa, ka, va, lam_q1[l], lam_k1[l], lam_q2[l], lam_k2[l], subln_g[l], lam_init,
                             batch, seq, tq, tk)
        yb = _neighbourhood_attention(qb, kb, vb, _na_bias_table(rpb[l]), batch, seq, NA_WIN_ROWS)
        h = _merge(h, ya, yb, gb, u, gates, sc_conv_w[l], wcast(w_branch[l]), wcast(w_mix_out[l]),
                   ln_g[l, 0], ln_b[l, 0], seq, tm, alpha)
        kv = _mem_kv(mem2d, wcast(xa_kv[l]))
        h = _cross_attention(h, kv, wcast(xa_q[l]), wcast(xa_o[l]), ln_g[l, 1], ln_b[l, 1], seq, mem_len, tm, alpha)
        h = _conv_ffn(h, wcast(_ffn_group_columns(ffn_w_in[l], FFN_CHUNK)), ffn_conv_w[l], ffn_conv_b[l], wcast(ffn_w_out[l]),
                      ln_g[l, 2], ln_b[l, 2], seq, tm, alpha)
    return h.reshape(batch, seq, d)
```

```python
import functools
import math

import numpy as np
import jax
import jax.numpy as jnp
from jax import lax
from jax.experimental import pallas as pl
from jax.experimental.pallas import tpu as pltpu

GRID_W = 64
HEAD_DIM = 64
DA_HEADS = 4
DA_VDIM = 2 * HEAD_DIM
NA_HEADS = 8
NA_WIN_ROWS = 8
NA_WIN_COLS = 16
SC_W = 512
N_BRANCH = 3
XA_HEADS = 4
ROPE_THETA = 10000.0
LN_EPS = 1e-5

LANES = 128
MXU_DTYPE = jnp.bfloat16
ACT_DTYPE = jnp.bfloat16
NEG_BIG = -1e30
VMEM_LIMIT = 56 * 1024 * 1024
FFN_CHUNK = 256
NA_ROWS_PER_STEP = 16


def _params(*sem):
    return pltpu.CompilerParams(dimension_semantics=sem, vmem_limit_bytes=VMEM_LIMIT)


def _resident(shape, index_map):
    return pl.BlockSpec(shape, index_map, pipeline_mode=pl.Buffered(1))


def _layer_norm(v, g, b):
    mu = jnp.mean(v, axis=-1, keepdims=True)
    d = v - mu
    var = jnp.mean(d * d, axis=-1, keepdims=True)
    return d * lax.rsqrt(var + LN_EPS) * g + b


def _dot(a, b):
    return jnp.dot(a.astype(MXU_DTYPE), b.astype(MXU_DTYPE), preferred_element_type=jnp.float32)


def _dot_nt(a, b):
    return lax.dot_general(a.astype(MXU_DTYPE), b.astype(MXU_DTYPE), (((1,), (1,)), ((), ())),
                           preferred_element_type=jnp.float32)


def _proj_kernel(*refs, d_gate, embed_ln):
    if embed_ln:
        x_ref, g_ref, b_ref, w_ref, rope_ref, xn_ref, *outs = refs
        x = _layer_norm(x_ref[...], g_ref[...], b_ref[...])
        xn_ref[...] = x
    else:
        x_ref, w_ref, rope_ref, *outs = refs
        x = x_ref[...]
    qa_ref, ka_ref, va_ref, qb_ref, kb_ref, vb_ref, gb_ref, u_ref, gate_ref = outs
    xb = x.astype(MXU_DTYPE)
    tm = xb.shape[0]
    w = SC_W

    def mm(c0, c1):
        return jnp.dot(xb, w_ref[:, c0:c1], preferred_element_type=jnp.float32)

    lane = lax.broadcasted_iota(jnp.int32, (tm, LANES), 1)
    first_half = (lane % HEAD_DIM) < (HEAD_DIM // 2)

    def rope(y, cos, sin):
        outs = []
        for c in range(y.shape[1] // LANES):
            yc = y[:, c * LANES:(c + 1) * LANES]
            rot = jnp.where(first_half, pltpu.roll(yc, LANES - HEAD_DIM // 2, 1),
                            pltpu.roll(yc, HEAD_DIM // 2, 1))
            outs.append(yc * cos + rot * sin)
        return jnp.concatenate(outs, axis=1)

    tab = rope_ref[...]
    qa_ref[...] = rope(mm(0, w), tab[:, 0:LANES], tab[:, LANES:2 * LANES]).astype(qa_ref.dtype)
    ka_ref[...] = rope(mm(w, 2 * w), tab[:, 2 * LANES:3 * LANES], tab[:, 3 * LANES:4 * LANES]).astype(ka_ref.dtype)
    va_ref[...] = mm(2 * w, 3 * w).astype(va_ref.dtype)
    qb_ref[...] = (mm(3 * w, 4 * w) * (HEAD_DIM ** -0.5)).astype(qb_ref.dtype)
    kb_ref[...] = mm(4 * w, 5 * w).astype(kb_ref.dtype)
    vb_ref[...] = mm(5 * w, 6 * w).astype(vb_ref.dtype)
    gb_ref[...] = mm(6 * w, 7 * w).astype(gb_ref.dtype)
    u_ref[...] = (mm(7 * w, 8 * w) * mm(8 * w, 9 * w)).astype(u_ref.dtype)
    gate_ref[...] = jax.nn.sigmoid(mm(9 * w, 9 * w + d_gate)).astype(gate_ref.dtype)


def _input_proj(x2d, w_in, rope_tab, seq, tm, emb_ln=None):
    t, d = x2d.shape
    n_cols = w_in.shape[1]
    d_gate = n_cols - 9 * SC_W
    s_blocks = seq // tm
    row = lambda i: (i, 0)
    const2 = lambda i: (0, 0)
    small = jax.ShapeDtypeStruct((t, SC_W), ACT_DTYPE)
    ln_args, ln_specs, ln_shape, ln_out = [], [], [], []
    if emb_ln is not None:
        ln_args = [a.reshape(1, d) for a in emb_ln]
        ln_specs = [pl.BlockSpec((1, d), const2)] * 2
        ln_shape = [jax.ShapeDtypeStruct((t, d), jnp.float32)]
        ln_out = [pl.BlockSpec((tm, d), row)]
    return pl.pallas_call(
        functools.partial(_proj_kernel, d_gate=d_gate, embed_ln=emb_ln is not None),
        out_shape=ln_shape + [small] * 8 + [jax.ShapeDtypeStruct((t, d_gate), ACT_DTYPE)],
        grid=(t // tm,),
        in_specs=[pl.BlockSpec((tm, d), row)] + ln_specs + [
            _resident((d, n_cols), const2),
            pl.BlockSpec((tm, 4 * LANES), lambda i: (i % s_blocks, 0))],
        out_specs=ln_out + [pl.BlockSpec((tm, SC_W), row)] * 8 + [pl.BlockSpec((tm, d_gate), row)],
        compiler_params=_params("parallel"),
        name="input_proj",
    )(x2d, *ln_args, w_in, rope_tab)


def _rope_table(seq):
    half = HEAD_DIM // 2
    inv = ROPE_THETA ** (-jnp.arange(half, dtype=jnp.float32) * 2.0 / HEAD_DIM)
    ang = jnp.arange(seq, dtype=jnp.float32)[:, None] * inv[None, :]
    cos, sin = jnp.cos(ang), jnp.sin(ang)
    cos_l = jnp.tile(jnp.concatenate([cos, cos], axis=1), (1, LANES // HEAD_DIM))
    sin_l = jnp.tile(jnp.concatenate([-sin, sin], axis=1), (1, LANES // HEAD_DIM))
    scale = HEAD_DIM ** -0.5
    return jnp.concatenate([cos_l * scale, sin_l * scale, cos_l, sin_l], axis=1)


def _diff_attn_kernel(q_ref, k_ref, v_ref, lq1_ref, lk1_ref, lq2_ref, lk2_ref, g_ref, o_ref,
                      qs_ref, m_ref, acc_ref, *buf_refs, tq, tk, rc, n_kv, n_q, ahead, lam_init):
    lane = lax.broadcasted_iota(jnp.int32, (tq, LANES), 1)
    ones_col = jnp.where(lax.broadcasted_iota(jnp.int32, (tk, LANES), 1) == 0, 1.0, 0.0).astype(v_ref.dtype)
    lam = (jnp.exp(jnp.sum(lq1_ref[...] * lk1_ref[...], axis=1, keepdims=True))
           - jnp.exp(jnp.sum(lq2_ref[...] * lk2_ref[...], axis=1, keepdims=True)) + lam_init)
    n_buf = len(buf_refs) // 2
    s_bufs = tuple(zip(buf_refs[:n_buf], buf_refs[n_buf:]))

    def stack_queries(t):
        q = q_ref[pl.ds(pl.multiple_of(t * tq, tq), tq), :]
        zero = jnp.zeros_like(q)
        qs_ref[0:tq, :] = jnp.where(lane < HEAD_DIM, q, zero)
        qs_ref[tq:2 * tq, :] = jnp.where(lane >= HEAD_DIM, q, zero)

    def scores(j):
        s_ref, mx_ref = s_bufs[j % n_buf]
        s = _dot_nt(qs_ref[...], k_ref[j * tk:(j + 1) * tk, :])
        s_ref[...] = s
        mx_ref[...] = functools.reduce(jnp.maximum, [s[:, c * LANES:(c + 1) * LANES] for c in range(tk // LANES)])

    def update(j):
        s_ref, mx_ref = s_bufs[j % n_buf]
        v_ext = jnp.concatenate([v_ref[j * tk:(j + 1) * tk, :], ones_col], axis=1)
        for c in range(2 * tq // rc):
            rows = slice(c * rc, (c + 1) * rc)
            s = s_ref[rows, :]
            m_prev = m_ref[rows, :]
            m_new = jnp.maximum(m_prev, jnp.max(mx_ref[rows, :], axis=1, keepdims=True))
            p = jnp.exp(s - jnp.concatenate([m_new] * (tk // LANES), axis=1))
            alpha = jnp.exp(m_prev - m_new)
            pv = _dot(p, v_ext)
            acc_ref[rows, :] = acc_ref[rows, :] * jnp.concatenate([alpha, alpha], axis=1) + pv
            m_ref[rows, :] = m_new

    def finalize(t):
        acc = acc_ref[...]
        o = acc[:, 0:LANES] / acc[:, LANES:LANES + 1]
        od = o[0:tq] - lam * o[tq:2 * tq]
        ms = jnp.mean(od * od, axis=-1, keepdims=True)
        y = od * lax.rsqrt(ms + LN_EPS) * g_ref[...] * (1.0 - lam_init)
        o_ref[pl.ds(pl.multiple_of(t * tq, tq), tq), :] = y.astype(o_ref.dtype)

    stack_queries(0)
    for j in range(ahead):
        scores(j)

    def tile(t, carry):
        m_ref[...] = jnp.full(m_ref.shape, NEG_BIG, jnp.float32)
        acc_ref[...] = jnp.zeros(acc_ref.shape, jnp.float32)
        for j in range(n_kv):
            if j + ahead == n_kv:
                stack_queries(jnp.minimum(t + 1, n_q - 1))
            scores((j + ahead) % n_kv)
            update(j)
        finalize(t)
        return carry

    lax.fori_loop(0, n_q, tile, 0)


def _diff_attention(qa, ka, va, lq1, lk1, lq2, lk2, subln_g, lam_init, batch, seq, tq, tk):
    t = qa.shape[0]
    n_q = seq // tq
    n_kv = seq // tk
    rc = 256
    ahead = 1
    n_buf = 2 * ahead
    assert n_kv % n_buf == 0 and (2 * tq) % rc == 0
    vec = lambda a: a.reshape(1, -1)
    vspec = lambda n: pl.BlockSpec((1, n), lambda b, h: (0, 0))
    head = pl.BlockSpec((seq, LANES), lambda b, h: (b, h))
    return pl.pallas_call(
        functools.partial(_diff_attn_kernel, tq=tq, tk=tk, rc=rc, n_kv=n_kv, n_q=n_q, ahead=ahead,
                          lam_init=lam_init),
        out_shape=jax.ShapeDtypeStruct((t, DA_HEADS * DA_VDIM), ACT_DTYPE),
        grid=(batch, DA_HEADS),
        in_specs=[head, head, head,
                  vspec(HEAD_DIM), vspec(HEAD_DIM), vspec(HEAD_DIM), vspec(HEAD_DIM), vspec(DA_VDIM)],
        out_specs=head,
        scratch_shapes=([pltpu.VMEM((2 * tq, LANES), qa.dtype),
                         pltpu.VMEM((2 * tq, LANES), jnp.float32),
                         pltpu.VMEM((2 * tq, 2 * LANES), jnp.float32)]
                        + [pltpu.VMEM((2 * tq, tk), jnp.float32)] * n_buf
                        + [pltpu.VMEM((2 * tq, LANES), jnp.float32)] * n_buf),
        compiler_params=_params("parallel", "parallel"),
        name="diff_attention",
    )(qa, ka, va, vec(lq1), vec(lk1), vec(lq2), vec(lk2), vec(subln_g))


def _na_bias_table(rpb):
    nh, n_dr, n_dc = rpb.shape
    qc = np.arange(GRID_W)
    cs = np.clip(qc - NA_WIN_COLS // 2, 0, GRID_W - NA_WIN_COLS)
    kc = np.arange(GRID_W)
    valid = (kc[None, :] >= cs[:, None]) & (kc[None, :] < cs[:, None] + NA_WIN_COLS)
    p = GRID_W + NA_WIN_COLS
    row = jnp.pad(rpb.astype(jnp.float32), ((0, 0), (0, 0), (0, p - n_dc)))
    toe = jnp.tile(row, (1, 1, GRID_W))[:, :, :GRID_W * (p - 1)].reshape(nh, n_dr, GRID_W, p - 1)
    toe = toe[:, :, :, NA_WIN_COLS - 1:NA_WIN_COLS - 1 + GRID_W]
    toe = jnp.where(jnp.asarray(valid)[None, None], toe, NEG_BIG)
    last = NA_WIN_ROWS - 1
    tab = jnp.stack([toe[:, last - k:last - k + NA_WIN_ROWS] for k in range(NA_WIN_ROWS)], axis=1)
    hp = LANES // HEAD_DIM
    tab = tab.reshape(nh // hp, hp, NA_WIN_ROWS, NA_WIN_ROWS, GRID_W, GRID_W).transpose(0, 2, 3, 5, 1, 4)
    return tab.reshape(nh // hp, NA_WIN_ROWS, NA_WIN_ROWS * GRID_W, hp * GRID_W)


def _na_kernel(q_ref, k_ref, v_ref, bias_ref, o_ref, *, rows, rows_per_step):
    i = pl.program_id(2)
    win = NA_WIN_ROWS * GRID_W
    lane = lax.broadcasted_iota(jnp.int32, (GRID_W, LANES), 1)
    lo = lane < HEAD_DIM
    for rr in range(rows_per_step):
        r = i * rows_per_step + rr
        rs = jnp.clip(r - NA_WIN_ROWS // 2, 0, rows - NA_WIN_ROWS)
        kind = jnp.where(r < NA_WIN_ROWS // 2, r,
                         jnp.where(r > rows - NA_WIN_ROWS // 2, r - (rows - NA_WIN_ROWS), NA_WIN_ROWS // 2))
        off = pl.multiple_of(rs * GRID_W, GRID_W)
        kwin = k_ref[pl.ds(off, win), :]
        vwin = v_ref[pl.ds(off, win), :]
        q = q_ref[rr * GRID_W:(rr + 1) * GRID_W, :]
        zero = jnp.zeros_like(q)
        qst = jnp.concatenate([jnp.where(lo, q, zero), jnp.where(lo, zero, q)], axis=0)
        st = _dot_nt(kwin, qst) + bias_ref[0, kind]
        m = jnp.max(st, axis=0, keepdims=True)
        p = jnp.exp(st - m)
        l = jnp.sum(p, axis=0, keepdims=True)
        pn = (p * (1.0 / l)).astype(MXU_DTYPE)
        pv = lax.dot_general(pn, vwin.astype(MXU_DTYPE), (((0,), (0,)), ((), ())),
                             preferred_element_type=jnp.float32)
        o = jnp.where(lo, pv[0:GRID_W], pv[GRID_W:2 * GRID_W])
        o_ref[rr * GRID_W:(rr + 1) * GRID_W, :] = o.astype(o_ref.dtype)


def _neighbourhood_attention(qb, kb, vb, bias_tab, batch, seq, rows_per_step):
    t = qb.shape[0]
    rows = seq // GRID_W
    tq = rows_per_step * GRID_W
    n_q = seq // tq
    pairs = NA_HEADS * HEAD_DIM // LANES
    return pl.pallas_call(
        functools.partial(_na_kernel, rows=rows, rows_per_step=rows_per_step),
        out_shape=jax.ShapeDtypeStruct((t, NA_HEADS * HEAD_DIM), ACT_DTYPE),
        grid=(batch, pairs, n_q),
        in_specs=[pl.BlockSpec((tq, LANES), lambda b, h, i: (b * n_q + i, h)),
                  pl.BlockSpec((seq, LANES), lambda b, h, i: (b, h)),
                  pl.BlockSpec((seq, LANES), lambda b, h, i: (b, h)),
                  pl.BlockSpec((1, NA_WIN_ROWS, NA_WIN_ROWS * GRID_W, LANES), lambda b, h, i: (h, 0, 0, 0))],
        out_specs=pl.BlockSpec((tq, LANES), lambda b, h, i: (b * n_q + i, h)),
        compiler_params=_params("parallel", "parallel", "parallel"),
        name="neighbourhood_attention",
    )(qb, kb, vb, bias_tab)


def _shift_rows(a, prev_row, next_row):
    n = a.shape[0]
    ridx = lax.broadcasted_iota(jnp.int32, a.shape, 0)
    up = jnp.where(ridx == 0, prev_row, pltpu.roll(a, 1, 0))
    dn = jnp.where(ridx == n - 1, next_row, pltpu.roll(a, n - 1, 0))
    return up, dn


def _merge_kernel(x_ref, ya_ref, yb_ref, gb_ref, u_ref, up_ref, un_ref, gate_ref, cw_ref, wb_ref, wm_ref,
                  g_ref, b_ref, o_ref, *, s_blocks, halo, alpha, d_model):
    i = pl.program_id(0)
    has_prev = i % s_blocks != 0
    has_next = i % s_blocks != s_blocks - 1
    u = u_ref[...].astype(jnp.float32)
    prev_row = jnp.where(has_prev, up_ref[halo - 1:halo, :].astype(jnp.float32), 0.0)
    next_row = jnp.where(has_next, un_ref[0:1, :].astype(jnp.float32), 0.0)
    u_up, u_dn = _shift_rows(u, prev_row, next_row)
    cw = cw_ref[...]
    conv = u_up * cw[0:1, :] + u * cw[1:2, :] + u_dn * cw[2:3, :]
    yc = gb_ref[...].astype(jnp.float32) * conv
    gate = gate_ref[...].astype(jnp.float32)
    merged = (gate[:, 0:d_model] * _dot(ya_ref[...], wb_ref[0])
              + gate[:, d_model:2 * d_model] * _dot(yb_ref[...], wb_ref[1])
              + gate[:, 2 * d_model:3 * d_model] * _dot(yc, wb_ref[2]))
    h = _dot(merged, wm_ref[...])
    o_ref[...] = _layer_norm(alpha * x_ref[...] + h, g_ref[...], b_ref[...])


def _merge(x2d, ya, yb, gb, u, gates, conv_w, w_branch, w_mix, ln_g, ln_b, seq, tm, alpha):
    t, d = x2d.shape
    halo = 16
    hb = tm // halo
    n_halo = t // halo
    row = lambda i: (i, 0)
    const2 = lambda i: (0, 0)
    return pl.pallas_call(
        functools.partial(_merge_kernel, s_blocks=seq // tm, halo=halo, alpha=alpha, d_model=d),
        out_shape=jax.ShapeDtypeStruct((t, d), jnp.float32),
        grid=(t // tm,),
        in_specs=[pl.BlockSpec((tm, d), row),
                  pl.BlockSpec((tm, SC_W), row),
                  pl.BlockSpec((tm, SC_W), row),
                  pl.BlockSpec((tm, SC_W), row),
                  pl.BlockSpec((tm, SC_W), row),
                  pl.BlockSpec((halo, SC_W), lambda i: (jnp.maximum(i * hb - 1, 0), 0)),
                  pl.BlockSpec((halo, SC_W), lambda i: (jnp.minimum((i + 1) * hb, n_halo - 1), 0)),
                  pl.BlockSpec((tm, N_BRANCH * d), row),
                  pl.BlockSpec(conv_w.shape, const2),
                  _resident(w_branch.shape, lambda i: (0, 0, 0)),
                  _resident(w_mix.shape, const2),
                  pl.BlockSpec((1, d), const2),
                  pl.BlockSpec((1, d), const2)],
        out_specs=pl.BlockSpec((tm, d), row),
        compiler_params=_params("parallel"),
        name="branch_merge",
    )(x2d, ya, yb, gb, u, u, u, gates, conv_w, w_branch, w_mix, ln_g.reshape(1, d), ln_b.reshape(1, d))


def _kv_kernel(m_ref, w_ref, o_ref):
    o_ref[...] = _dot(m_ref[...], w_ref[...]).astype(o_ref.dtype)


def _mem_kv(mem2d, xa_kv):
    t, d = mem2d.shape
    n = xa_kv.shape[1]
    tm = min(t, 256)
    return pl.pallas_call(
        _kv_kernel,
        out_shape=jax.ShapeDtypeStruct((t, n), ACT_DTYPE),
        grid=(t // tm,),
        in_specs=[pl.BlockSpec((tm, d), lambda i: (i, 0)), _resident((d, n), lambda i: (0, 0))],
        out_specs=pl.BlockSpec((tm, n), lambda i: (i, 0)),
        compiler_params=_params("parallel"),
        name="mem_kv",
    )(mem2d, xa_kv)


def _xattn_kernel(x_ref, k_ref, v_ref, wq_ref, wo_ref, g_ref, b_ref, o_ref, *, alpha):
    x = x_ref[...]
    d = x.shape[1]
    hd = d // XA_HEADS
    q = _dot(x, wq_ref[...]) * (hd ** -0.5)
    k = k_ref[...]
    v = v_ref[...]
    outs = []
    for h in range(XA_HEADS):
        sl = slice(h * hd, (h + 1) * hd)
        s = _dot_nt(q[:, sl], k[:, sl])
        m = jnp.max(s, axis=1, keepdims=True)
        p = jnp.exp(s - m)
        l = jnp.sum(p, axis=1, keepdims=True)
        outs.append(_dot(p, v[:, sl]) / l)
    o = jnp.concatenate(outs, axis=1)
    h_out = _dot(o, wo_ref[...])
    o_ref[...] = _layer_norm(alpha * x + h_out, g_ref[...], b_ref[...])


def _cross_attention(x2d, kv, xa_q, xa_o, ln_g, ln_b, seq, mem_len, tm, alpha):
    t, d = x2d.shape
    s_blocks = seq // tm
    row = lambda i: (i, 0)
    const2 = lambda i: (0, 0)
    return pl.pallas_call(
        functools.partial(_xattn_kernel, alpha=alpha),
        out_shape=jax.ShapeDtypeStruct((t, d), jnp.float32),
        grid=(t // tm,),
        in_specs=[pl.BlockSpec((tm, d), row),
                  pl.BlockSpec((mem_len, d), lambda i: (i // s_blocks, 0)),
                  pl.BlockSpec((mem_len, d), lambda i: (i // s_blocks, 1)),
                  _resident((d, d), const2),
                  _resident((d, d), const2),
                  pl.BlockSpec((1, d), const2),
                  pl.BlockSpec((1, d), const2)],
        out_specs=pl.BlockSpec((tm, d), row),
        compiler_params=_params("parallel"),
        name="cross_attention",
    )(x2d, kv, kv, xa_q, xa_o, ln_g.reshape(1, d), ln_b.reshape(1, d))


def _ffn_kernel(x_ref, xp_ref, xn_ref, wi_ref, cw_ref, cb_ref, wo_ref, g_ref, b_ref, o_ref, hid_ref,
                *, s_blocks, halo, alpha, d_ff, tf):
    i = pl.program_id(0)
    has_prev = i % s_blocks != 0
    has_next = i % s_blocks != s_blocks - 1
    x = x_ref[...]
    xb = x.astype(MXU_DTYPE)
    xe = jnp.concatenate([jnp.where(has_prev, xp_ref[halo - 1:halo, :], 0.0),
                          jnp.where(has_next, xn_ref[0:1, :], 0.0),
                          jnp.zeros((halo - 2, x.shape[1]), jnp.float32)], axis=0).astype(MXU_DTYPE)
    for c in range(d_ff // tf):
        w_c = jnp.concatenate([wi_ref[:, c * tf:(c + 1) * tf], wi_ref[:, d_ff + c * tf:d_ff + (c + 1) * tf]], axis=1)
        ug = jnp.dot(xb, w_c, preferred_element_type=jnp.float32)
        u, gt = ug[:, 0:tf], ug[:, tf:2 * tf]
        ge = jnp.dot(xe, w_c, preferred_element_type=jnp.float32)[:, tf:2 * tf]
        g_up, g_dn = _shift_rows(gt, ge[0:1, :], ge[1:2, :])
        cw = cw_ref[:, c * tf:(c + 1) * tf]
        a = g_up * cw[0:1, :] + gt * cw[1:2, :] + g_dn * cw[2:3, :] + cb_ref[:, c * tf:(c + 1) * tf]
        hid_ref[:, c * tf:(c + 1) * tf] = ((a * jax.nn.sigmoid(a)) * u).astype(hid_ref.dtype)
    h = jnp.dot(hid_ref[...], wo_ref[...], preferred_element_type=jnp.float32)
    o_ref[...] = _layer_norm(alpha * x + h, g_ref[...], b_ref[...])


def _conv_ffn(x2d, w_in, conv_w, conv_b, w_out, ln_g, ln_b, seq, tm, alpha):
    t, d = x2d.shape
    d_ff = w_out.shape[0]
    tf = FFN_CHUNK
    halo = 8
    hb = tm // halo
    n_halo = t // halo
    row = lambda i: (i, 0)
    const2 = lambda i: (0, 0)
    return pl.pallas_call(
        functools.partial(_ffn_kernel, s_blocks=seq // tm, halo=halo, alpha=alpha, d_ff=d_ff, tf=tf),
        out_shape=jax.ShapeDtypeStruct((t, d), jnp.float32),
        grid=(t // tm,),
        in_specs=[pl.BlockSpec((tm, d), row),
                  pl.BlockSpec((halo, d), lambda i: (jnp.maximum(i * hb - 1, 0), 0)),
                  pl.BlockSpec((halo, d), lambda i: (jnp.minimum((i + 1) * hb, n_halo - 1), 0)),
                  _resident(w_in.shape, const2),
                  pl.BlockSpec(conv_w.shape, const2),
                  pl.BlockSpec((1, d_ff), const2),
                  _resident(w_out.shape, const2),
                  pl.BlockSpec((1, d), const2),
                  pl.BlockSpec((1, d), const2)],
        out_specs=pl.BlockSpec((tm, d), row),
        scratch_shapes=[pltpu.VMEM((tm, d_ff), MXU_DTYPE)],
        compiler_params=_params("parallel"),
        name="conv_ffn",
    )(x2d, x2d, x2d, w_in, conv_w, conv_b.reshape(1, d_ff), w_out, ln_g.reshape(1, d), ln_b.reshape(1, d))

def kernel(x, mem, emb_ln_g, emb_ln_b, w_in, lam_q1, lam_k1, lam_q2, lam_k2, subln_g, rpb, sc_conv_w, w_branch,
           w_mix_out, xa_q, xa_kv, xa_o, ffn_w_in, ffn_conv_w, ffn_conv_b, ffn_w_out, ln_g, ln_b):
    batch, seq, d = x.shape
    depth = w_in.shape[0]
    mem_len = mem.shape[1]
    assert seq % GRID_W == 0 and seq // GRID_W >= NA_WIN_ROWS
    alpha = (2.0 * depth) ** 0.25
    tm = min(512, seq)
    tq = min(512, seq)
    tk = min(1024, seq // 2)
    wcast = lambda a: a.astype(MXU_DTYPE)

    rope_tab = _rope_table(seq)
    h = x.reshape(batch * seq, d)
    mem2d = mem.reshape(batch * mem_len, d)
    for l in range(depth):
        lam_init = 0.8 - 0.6 * math.exp(-0.3 * l)
        if l == 0:
            h, *proj = _input_proj(h, wcast(w_in[l]), rope_tab, seq, tm, emb_ln=(emb_ln_g, emb_ln_b))
        else:
            proj = _input_proj(h, wcast(w_in[l]), rope_tab, seq, tm)
        qa, ka, va, qb, kb, vb, gb, u, gates = proj
        ya = _diff_attention(qa, ka, va, lam_q1[l], lam_k1[l], lam_q2[l], lam_k2[l], subln_g[l], lam_init,
                             batch, seq, tq, tk)
        yb = _neighbourhood_attention(qb, kb, vb, _na_bias_table(rpb[l]), batch, seq, NA_ROWS_PER_STEP)
        h = _merge(h, ya, yb, gb, u, gates, sc_conv_w[l], wcast(w_branch[l]), wcast(w_mix_out[l]),
                   ln_g[l, 0], ln_b[l, 0], seq, tm, alpha)
        kv = _mem_kv(mem2d, wcast(xa_kv[l]))
        h = _cross_attention(h, kv, wcast(xa_q[l]), wcast(xa_o[l]), ln_g[l, 1], ln_b[l, 1], seq, mem_len, tm, alpha)
        h = _conv_ffn(h, wcast(ffn_w_in[l]), ffn_conv_w[l], ffn_conv_b[l], wcast(ffn_w_out[l]),
                      ln_g[l, 2], ln_b[l, 2], seq, tm, alpha)
    return h.reshape(batch, seq, d)
```

```python
import functools
import math

import numpy as np
import jax
import jax.numpy as jnp
from jax import lax
from jax.experimental import pallas as pl
from jax.experimental.pallas import tpu as pltpu

GRID_W = 64
HEAD_DIM = 64
DA_HEADS = 4
DA_VDIM = 2 * HEAD_DIM
NA_HEADS = 8
NA_WIN_ROWS = 8
NA_WIN_COLS = 16
SC_W = 512
N_BRANCH = 3
XA_HEADS = 4
ROPE_THETA = 10000.0
LN_EPS = 1e-5

LANES = 128
MXU_DTYPE = jnp.bfloat16
ACT_DTYPE = jnp.bfloat16
NEG_BIG = -1e30
VMEM_LIMIT = 56 * 1024 * 1024
FFN_CHUNK = 256
NA_ROWS_PER_STEP = 16


def _params(*sem):
    return pltpu.CompilerParams(dimension_semantics=sem, vmem_limit_bytes=VMEM_LIMIT)


def _resident(shape, index_map):
    return pl.BlockSpec(shape, index_map, pipeline_mode=pl.Buffered(1))


def _layer_norm(v, g, b):
    mu = jnp.mean(v, axis=-1, keepdims=True)
    d = v - mu
    var = jnp.mean(d * d, axis=-1, keepdims=True)
    return d * lax.rsqrt(var + LN_EPS) * g + b


def _dot(a, b):
    return jnp.dot(a.astype(MXU_DTYPE), b.astype(MXU_DTYPE), preferred_element_type=jnp.float32)


def _dot_nt(a, b):
    return lax.dot_general(a.astype(MXU_DTYPE), b.astype(MXU_DTYPE), (((1,), (1,)), ((), ())),
                           preferred_element_type=jnp.float32)


def _proj_kernel(*refs, d_gate, embed_ln):
    if embed_ln:
        x_ref, g_ref, b_ref, w_ref, rope_ref, xn_ref, *outs = refs
        x = _layer_norm(x_ref[...], g_ref[...], b_ref[...])
        xn_ref[...] = x
    else:
        x_ref, w_ref, rope_ref, *outs = refs
        x = x_ref[...]
    qa_ref, ka_ref, va_ref, qb_ref, kb_ref, vb_ref, gb_ref, u_ref, gate_ref = outs
    xb = x.astype(MXU_DTYPE)
    tm = xb.shape[0]
    w = SC_W

    def mm(c0, c1):
        return jnp.dot(xb, w_ref[:, c0:c1], preferred_element_type=jnp.float32)

    lane = lax.broadcasted_iota(jnp.int32, (tm, LANES), 1)
    first_half = (lane % HEAD_DIM) < (HEAD_DIM // 2)

    def rope(y, cos, sin):
        outs = []
        for c in range(y.shape[1] // LANES):
            yc = y[:, c * LANES:(c + 1) * LANES]
            rot = jnp.where(first_half, pltpu.roll(yc, LANES - HEAD_DIM // 2, 1),
                            pltpu.roll(yc, HEAD_DIM // 2, 1))
            outs.append(yc * cos + rot * sin)
        return jnp.concatenate(outs, axis=1)

    tab = rope_ref[...]
    qa_ref[...] = rope(mm(0, w), tab[:, 0:LANES], tab[:, LANES:2 * LANES]).astype(qa_ref.dtype)
    ka_ref[...] = rope(mm(w, 2 * w), tab[:, 2 * LANES:3 * LANES], tab[:, 3 * LANES:4 * LANES]).astype(ka_ref.dtype)
    va_ref[...] = mm(2 * w, 3 * w).astype(va_ref.dtype)
    qb_ref[...] = (mm(3 * w, 4 * w) * (HEAD_DIM ** -0.5)).astype(qb_ref.dtype)
    kb_ref[...] = mm(4 * w, 5 * w).astype(kb_ref.dtype)
    vb_ref[...] = mm(5 * w, 6 * w).astype(vb_ref.dtype)
    gb_ref[...] = mm(6 * w, 7 * w).astype(gb_ref.dtype)
    u_ref[...] = (mm(7 * w, 8 * w) * mm(8 * w, 9 * w)).astype(u_ref.dtype)
    gate_ref[...] = jax.nn.sigmoid(mm(9 * w, 9 * w + d_gate)).astype(gate_ref.dtype)


def _input_proj(x2d, w_in, rope_tab, seq, tm, emb_ln=None):
    t, d = x2d.shape
    n_cols = w_in.shape[1]
    d_gate = n_cols - 9 * SC_W
    s_blocks = seq // tm
    row = lambda i: (i, 0)
    const2 = lambda i: (0, 0)
    small = jax.ShapeDtypeStruct((t, SC_W), ACT_DTYPE)
    ln_args, ln_specs, ln_shape, ln_out = [], [], [], []
    if emb_ln is not None:
        ln_args = [a.reshape(1, d) for a in emb_ln]
        ln_specs = [pl.BlockSpec((1, d), const2)] * 2
        ln_shape = [jax.ShapeDtypeStruct((t, d), jnp.float32)]
        ln_out = [pl.BlockSpec((tm, d), row)]
    return pl.pallas_call(
        functools.partial(_proj_kernel, d_gate=d_gate, embed_ln=emb_ln is not None),
        out_shape=ln_shape + [small] * 8 + [jax.ShapeDtypeStruct((t, d_gate), ACT_DTYPE)],
        grid=(t // tm,),
        in_specs=[pl.BlockSpec((tm, d), row)] + ln_specs + [
            _resident((d, n_cols), const2),
            pl.BlockSpec((tm, 4 * LANES), lambda i: (i % s_blocks, 0))],
        out_specs=ln_out + [pl.BlockSpec((tm, SC_W), row)] * 8 + [pl.BlockSpec((tm, d_gate), row)],
        compiler_params=_params("parallel"),
        name="input_proj",
    )(x2d, *ln_args, w_in, rope_tab)


def _rope_table(seq):
    half = HEAD_DIM // 2
    inv = ROPE_THETA ** (-jnp.arange(half, dtype=jnp.float32) * 2.0 / HEAD_DIM)
    ang = jnp.arange(seq, dtype=jnp.float32)[:, None] * inv[None, :]
    cos, sin = jnp.cos(ang), jnp.sin(ang)
    cos_l = jnp.tile(jnp.concatenate([cos, cos], axis=1), (1, LANES // HEAD_DIM))
    sin_l = jnp.tile(jnp.concatenate([-sin, sin], axis=1), (1, LANES // HEAD_DIM))
    scale = HEAD_DIM ** -0.5
    return jnp.concatenate([cos_l * scale, sin_l * scale, cos_l, sin_l], axis=1)


def _diff_attn_kernel(q_ref, k_ref, v_ref, lq1_ref, lk1_ref, lq2_ref, lk2_ref, g_ref, o_ref,
                      qs_ref, m_ref, acc_ref, *buf_refs, tq, tk, rc, n_kv, n_q, ahead, lam_init):
    lane = lax.broadcasted_iota(jnp.int32, (tq, LANES), 1)
    ones_col = jnp.where(lax.broadcasted_iota(jnp.int32, (tk, LANES), 1) == 0, 1.0, 0.0).astype(v_ref.dtype)
    lam = (jnp.exp(jnp.sum(lq1_ref[0] * lk1_ref[0], axis=1, keepdims=True))
           - jnp.exp(jnp.sum(lq2_ref[0] * lk2_ref[0], axis=1, keepdims=True)) + lam_init)
    n_buf = len(buf_refs) // 2
    s_bufs = tuple(zip(buf_refs[:n_buf], buf_refs[n_buf:]))

    def stack_queries(t):
        q = q_ref[pl.ds(pl.multiple_of(t * tq, tq), tq), :]
        zero = jnp.zeros_like(q)
        qs_ref[0:tq, :] = jnp.where(lane < HEAD_DIM, q, zero)
        qs_ref[tq:2 * tq, :] = jnp.where(lane >= HEAD_DIM, q, zero)

    def scores(j):
        s_ref, mx_ref = s_bufs[j % n_buf]
        s = _dot_nt(qs_ref[...], k_ref[j * tk:(j + 1) * tk, :])
        s_ref[...] = s
        mx_ref[...] = functools.reduce(jnp.maximum, [s[:, c * LANES:(c + 1) * LANES] for c in range(tk // LANES)])

    def update(j):
        s_ref, mx_ref = s_bufs[j % n_buf]
        v_ext = jnp.concatenate([v_ref[j * tk:(j + 1) * tk, :], ones_col], axis=1)
        for c in range(2 * tq // rc):
            rows = slice(c * rc, (c + 1) * rc)
            s = s_ref[rows, :]
            m_prev = m_ref[rows, :]
            m_new = jnp.maximum(m_prev, jnp.max(mx_ref[rows, :], axis=1, keepdims=True))
            p = jnp.exp(s - jnp.concatenate([m_new] * (tk // LANES), axis=1))
            alpha = jnp.exp(m_prev - m_new)
            pv = _dot(p, v_ext)
            acc_ref[rows, :] = acc_ref[rows, :] * jnp.concatenate([alpha, alpha], axis=1) + pv
            m_ref[rows, :] = m_new

    def finalize(t):
        acc = acc_ref[...]
        o = acc[:, 0:LANES] / acc[:, LANES:LANES + 1]
        od = o[0:tq] - lam * o[tq:2 * tq]
        ms = jnp.mean(od * od, axis=-1, keepdims=True)
        y = od * lax.rsqrt(ms + LN_EPS) * g_ref[0] * (1.0 - lam_init)
        o_ref[pl.ds(pl.multiple_of(t * tq, tq), tq), :] = y.astype(o_ref.dtype)

    stack_queries(0)
    for j in range(ahead):
        scores(j)

    def tile(t, carry):
        m_ref[...] = jnp.full(m_ref.shape, NEG_BIG, jnp.float32)
        acc_ref[...] = jnp.zeros(acc_ref.shape, jnp.float32)
        for j in range(n_kv):
            if j + ahead == n_kv:
                stack_queries(jnp.minimum(t + 1, n_q - 1))
            scores((j + ahead) % n_kv)
            update(j)
        finalize(t)
        return carry

    lax.fori_loop(0, n_q, tile, 0)


def _diff_attention(qa, ka, va, lq1, lk1, lq2, lk2, subln_g, layer, lam_init, batch, seq, tq, tk):
    t = qa.shape[0]
    n_q = seq // tq
    n_kv = seq // tk
    rc = 256
    ahead = 1
    n_buf = 2 * ahead
    assert n_kv % n_buf == 0 and (2 * tq) % rc == 0
    vspec = lambda n: pl.BlockSpec((1, 1, n), lambda b, h: (layer, 0, 0))
    head = pl.BlockSpec((seq, LANES), lambda b, h: (b, h))
    return pl.pallas_call(
        functools.partial(_diff_attn_kernel, tq=tq, tk=tk, rc=rc, n_kv=n_kv, n_q=n_q, ahead=ahead,
                          lam_init=lam_init),
        out_shape=jax.ShapeDtypeStruct((t, DA_HEADS * DA_VDIM), ACT_DTYPE),
        grid=(batch, DA_HEADS),
        in_specs=[head, head, head,
                  vspec(HEAD_DIM), vspec(HEAD_DIM), vspec(HEAD_DIM), vspec(HEAD_DIM), vspec(DA_VDIM)],
        out_specs=head,
        scratch_shapes=([pltpu.VMEM((2 * tq, LANES), qa.dtype),
                         pltpu.VMEM((2 * tq, LANES), jnp.float32),
                         pltpu.VMEM((2 * tq, 2 * LANES), jnp.float32)]
                        + [pltpu.VMEM((2 * tq, tk), jnp.float32)] * n_buf
                        + [pltpu.VMEM((2 * tq, LANES), jnp.float32)] * n_buf),
        compiler_params=_params("parallel", "parallel"),
        name="diff_attention",
    )(qa, ka, va, lq1, lk1, lq2, lk2, subln_g)


def _na_bias_table(rpb):
    nh, n_dr, n_dc = rpb.shape
    qc = np.arange(GRID_W)
    cs = np.clip(qc - NA_WIN_COLS // 2, 0, GRID_W - NA_WIN_COLS)
    kc = np.arange(GRID_W)
    valid = (kc[None, :] >= cs[:, None]) & (kc[None, :] < cs[:, None] + NA_WIN_COLS)
    p = GRID_W + NA_WIN_COLS
    row = jnp.pad(rpb.astype(jnp.float32), ((0, 0), (0, 0), (0, p - n_dc)))
    toe = jnp.tile(row, (1, 1, GRID_W))[:, :, :GRID_W * (p - 1)].reshape(nh, n_dr, GRID_W, p - 1)
    toe = toe[:, :, :, NA_WIN_COLS - 1:NA_WIN_COLS - 1 + GRID_W]
    toe = jnp.where(jnp.asarray(valid)[None, None], toe, NEG_BIG)
    hp = LANES // HEAD_DIM
    tab = toe.reshape(nh // hp, hp, n_dr, GRID_W, GRID_W).transpose(0, 2, 4, 1, 3)
    return tab.reshape(nh // hp, n_dr * GRID_W, hp * GRID_W)


def _na_kernel(q_ref, k_ref, v_ref, bias_ref, o_ref, *, rows, rows_per_step):
    i = pl.program_id(2)
    win = NA_WIN_ROWS * GRID_W
    lane = lax.broadcasted_iota(jnp.int32, (GRID_W, LANES), 1)
    lo = lane < HEAD_DIM
    for rr in range(rows_per_step):
        r = i * rows_per_step + rr
        rs = jnp.clip(r - NA_WIN_ROWS // 2, 0, rows - NA_WIN_ROWS)
        off = pl.multiple_of(rs * GRID_W, GRID_W)
        bias_off = pl.multiple_of((rs - r + NA_WIN_ROWS - 1) * GRID_W, GRID_W)
        kwin = k_ref[pl.ds(off, win), :]
        vwin = v_ref[pl.ds(off, win), :]
        q = q_ref[rr * GRID_W:(rr + 1) * GRID_W, :]
        zero = jnp.zeros_like(q)
        qst = jnp.concatenate([jnp.where(lo, q, zero), jnp.where(lo, zero, q)], axis=0)
        st = _dot_nt(kwin, qst) + bias_ref[0, pl.ds(bias_off, win), :]
        m = jnp.max(st, axis=0, keepdims=True)
        p = jnp.exp(st - m)
        l = jnp.sum(p, axis=0, keepdims=True)
        pn = (p * (1.0 / l)).astype(MXU_DTYPE)
        pv = lax.dot_general(pn, vwin.astype(MXU_DTYPE), (((0,), (0,)), ((), ())),
                             preferred_element_type=jnp.float32)
        o = jnp.where(lo, pv[0:GRID_W], pv[GRID_W:2 * GRID_W])
        o_ref[rr * GRID_W:(rr + 1) * GRID_W, :] = o.astype(o_ref.dtype)


def _neighbourhood_attention(qb, kb, vb, bias_tab, layer, batch, seq, rows_per_step):
    t = qb.shape[0]
    rows = seq // GRID_W
    tq = rows_per_step * GRID_W
    n_q = seq // tq
    pairs = NA_HEADS * HEAD_DIM // LANES
    return pl.pallas_call(
        functools.partial(_na_kernel, rows=rows, rows_per_step=rows_per_step),
        out_shape=jax.ShapeDtypeStruct((t, NA_HEADS * HEAD_DIM), ACT_DTYPE),
        grid=(batch, pairs, n_q),
        in_specs=[pl.BlockSpec((tq, LANES), lambda b, h, i: (b * n_q + i, h)),
                  pl.BlockSpec((seq, LANES), lambda b, h, i: (b, h)),
                  pl.BlockSpec((seq, LANES), lambda b, h, i: (b, h)),
                  pl.BlockSpec((1,) + bias_tab.shape[1:], lambda b, h, i: (layer * pairs + h, 0, 0))],
        out_specs=pl.BlockSpec((tq, LANES), lambda b, h, i: (b * n_q + i, h)),
        compiler_params=_params("parallel", "parallel", "parallel"),
        name="neighbourhood_attention",
    )(qb, kb, vb, bias_tab)


def _shift_rows(a, prev_row, next_row):
    n = a.shape[0]
    ridx = lax.broadcasted_iota(jnp.int32, a.shape, 0)
    up = jnp.where(ridx == 0, prev_row, pltpu.roll(a, 1, 0))
    dn = jnp.where(ridx == n - 1, next_row, pltpu.roll(a, n - 1, 0))
    return up, dn


def _merge_xattn_kernel(x_ref, ya_ref, yb_ref, gb_ref, u_ref, up_ref, un_ref, gate_ref, cw_ref, wb_ref, wm_ref,
                        k_ref, v_ref, wq_ref, wo_ref, g_ref, b_ref, o_ref,
                        *, s_blocks, halo, alpha, d_model, n_sub):
    i = pl.program_id(0)
    has_prev = i % s_blocks != 0
    has_next = i % s_blocks != s_blocks - 1
    u = u_ref[...].astype(jnp.float32)
    prev_row = jnp.where(has_prev, up_ref[halo - 1:halo, :].astype(jnp.float32), 0.0)
    next_row = jnp.where(has_next, un_ref[0:1, :].astype(jnp.float32), 0.0)
    u_up, u_dn = _shift_rows(u, prev_row, next_row)
    cw = cw_ref[0]
    conv = u_up * cw[0:1, :] + u * cw[1:2, :] + u_dn * cw[2:3, :]
    yc = (gb_ref[...].astype(jnp.float32) * conv).astype(MXU_DTYPE)
    ln_g, ln_b = g_ref[0], b_ref[0]
    k = k_ref[...]
    v = v_ref[...]
    hd = d_model // XA_HEADS
    ts = x_ref.shape[0] // n_sub
    subs = [slice(sub * ts, (sub + 1) * ts) for sub in range(n_sub)]

    def merge(rows):
        gate = gate_ref[rows, :].astype(jnp.float32)
        merged = (gate[:, 0:d_model] * _dot(ya_ref[rows, :], wb_ref[0])
                  + gate[:, d_model:2 * d_model] * _dot(yb_ref[rows, :], wb_ref[1])
                  + gate[:, 2 * d_model:3 * d_model] * _dot(yc[rows, :], wb_ref[2]))
        return alpha * x_ref[rows, :] + _dot(merged, wm_ref[...])

    def attend(q):
        outs = []
        for h in range(XA_HEADS):
            sl = slice(h * hd, (h + 1) * hd)
            s = _dot_nt(q[:, sl], k[:, sl])
            m = jnp.max(s, axis=1, keepdims=True)
            p = jnp.exp(s - m)
            l = jnp.sum(p, axis=1, keepdims=True)
            outs.append(_dot(p, v[:, sl]) / l)
        return jnp.concatenate(outs, axis=1)

    pre = [merge(rows) for rows in subs]
    x1 = [_layer_norm(a, ln_g[0:1, :], ln_b[0:1, :]) for a in pre]
    q = [_dot(a, wq_ref[...]) * (hd ** -0.5) for a in x1]
    att = [attend(a) for a in q]
    h_out = [_dot(a, wo_ref[...]) for a in att]
    for rows, a, b in zip(subs, x1, h_out):
        o_ref[rows, :] = _layer_norm(alpha * a + b, ln_g[1:2, :], ln_b[1:2, :])


def _merge_xattn(x2d, ya, yb, gb, u, gates, conv_w, w_branch, w_mix, kv, xa_q, xa_o, ln_g, ln_b,
                 layer, seq, mem_len, tm, alpha):
    t, d = x2d.shape
    halo = 16
    hb = tm // halo
    n_halo = t // halo
    s_blocks = seq // tm
    row = lambda i: (i, 0)
    const2 = lambda i: (0, 0)
    per_layer = lambda a: pl.BlockSpec((1,) + a.shape[1:], lambda i: (layer,) + (0,) * (a.ndim - 1))
    return pl.pallas_call(
        functools.partial(_merge_xattn_kernel, s_blocks=s_blocks, halo=halo, alpha=alpha, d_model=d,
                          n_sub=2 if tm % 32 == 0 else 1),
        out_shape=jax.ShapeDtypeStruct((t, d), jnp.float32),
        grid=(t // tm,),
        in_specs=[pl.BlockSpec((tm, d), row),
                  pl.BlockSpec((tm, SC_W), row),
                  pl.BlockSpec((tm, SC_W), row),
                  pl.BlockSpec((tm, SC_W), row),
                  pl.BlockSpec((tm, SC_W), row),
                  pl.BlockSpec((halo, SC_W), lambda i: (jnp.maximum(i * hb - 1, 0), 0)),
                  pl.BlockSpec((halo, SC_W), lambda i: (jnp.minimum((i + 1) * hb, n_halo - 1), 0)),
                  pl.BlockSpec((tm, N_BRANCH * d), row),
                  per_layer(conv_w),
                  _resident(w_branch.shape, lambda i: (0, 0, 0)),
                  _resident(w_mix.shape, const2),
                  pl.BlockSpec((mem_len, d), lambda i: (i // s_blocks, 0)),
                  pl.BlockSpec((mem_len, d), lambda i: (i // s_blocks, 1)),
                  _resident((d, d), const2),
                  _resident((d, d), const2),
                  per_layer(ln_g),
                  per_layer(ln_b)],
        out_specs=pl.BlockSpec((tm, d), row),
        compiler_params=_params("parallel"),
        name="merge_xattn",
    )(x2d, ya, yb, gb, u, u, u, gates, conv_w, w_branch, w_mix, kv, kv, xa_q, xa_o, ln_g, ln_b)


def _kv_kernel(m_ref, w_ref, o_ref):
    o_ref[...] = _dot(m_ref[...], w_ref[...]).astype(o_ref.dtype)


def _mem_kv(mem2d, xa_kv):
    t, d = mem2d.shape
    n = xa_kv.shape[1]
    tm = min(t, 256)
    return pl.pallas_call(
        _kv_kernel,
        out_shape=jax.ShapeDtypeStruct((t, n), ACT_DTYPE),
        grid=(t // tm,),
        in_specs=[pl.BlockSpec((tm, d), lambda i: (i, 0)), _resident((d, n), lambda i: (0, 0))],
        out_specs=pl.BlockSpec((tm, n), lambda i: (i, 0)),
        compiler_params=_params("parallel"),
        name="mem_kv",
    )(mem2d, xa_kv)


def _ffn_kernel(x_ref, xp_ref, xn_ref, wi_ref, cw_ref, cb_ref, wo_ref, g_ref, b_ref, o_ref, hid_ref,
                *, s_blocks, halo, alpha, d_ff, tf):
    i = pl.program_id(0)
    has_prev = i % s_blocks != 0
    has_next = i % s_blocks != s_blocks - 1
    x = x_ref[...]
    xb = x.astype(MXU_DTYPE)
    cw_all = cw_ref[0]
    ln_g, ln_b = g_ref[0], b_ref[0]
    xe = jnp.concatenate([jnp.where(has_prev, xp_ref[halo - 1:halo, :], 0.0),
                          jnp.where(has_next, xn_ref[0:1, :], 0.0),
                          jnp.zeros((halo - 2, x.shape[1]), jnp.float32)], axis=0).astype(MXU_DTYPE)
    for c in range(d_ff // tf):
        cols = slice(c * tf, (c + 1) * tf)
        w_c = jnp.concatenate([wi_ref[:, cols], wi_ref[:, d_ff + c * tf:d_ff + (c + 1) * tf]], axis=1)
        ug = jnp.dot(xb, w_c, preferred_element_type=jnp.float32)
        u, gt = ug[:, 0:tf], ug[:, tf:2 * tf]
        ge = jnp.dot(xe, w_c, preferred_element_type=jnp.float32)[:, tf:2 * tf]
        g_up, g_dn = _shift_rows(gt, ge[0:1, :], ge[1:2, :])
        cw = cw_all[:, cols]
        a = g_up * cw[0:1, :] + gt * cw[1:2, :] + g_dn * cw[2:3, :] + cb_ref[0, :, cols]
        hid_ref[:, cols] = ((a * jax.nn.sigmoid(a)) * u).astype(hid_ref.dtype)
    h = jnp.dot(hid_ref[...], wo_ref[...], preferred_element_type=jnp.float32)
    o_ref[...] = _layer_norm(alpha * x + h, ln_g[2:3, :], ln_b[2:3, :])


def _conv_ffn(x2d, w_in, conv_w, conv_b, w_out, ln_g, ln_b, layer, seq, tm, alpha):
    t, d = x2d.shape
    d_ff = w_out.shape[0]
    tf = FFN_CHUNK
    halo = 8
    hb = tm // halo
    n_halo = t // halo
    row = lambda i: (i, 0)
    const2 = lambda i: (0, 0)
    per_layer = lambda a: pl.BlockSpec((1,) + a.shape[1:], lambda i: (layer,) + (0,) * (a.ndim - 1))
    return pl.pallas_call(
        functools.partial(_ffn_kernel, s_blocks=seq // tm, halo=halo, alpha=alpha, d_ff=d_ff, tf=tf),
        out_shape=jax.ShapeDtypeStruct((t, d), jnp.float32),
        grid=(t // tm,),
        in_specs=[pl.BlockSpec((tm, d), row),
                  pl.BlockSpec((halo, d), lambda i: (jnp.maximum(i * hb - 1, 0), 0)),
                  pl.BlockSpec((halo, d), lambda i: (jnp.minimum((i + 1) * hb, n_halo - 1), 0)),
                  _resident(w_in.shape, const2),
                  per_layer(conv_w),
                  per_layer(conv_b),
                  _resident(w_out.shape, const2),
                  per_layer(ln_g),
                  per_layer(ln_b)],
        out_specs=pl.BlockSpec((tm, d), row),
        scratch_shapes=[pltpu.VMEM((tm, d_ff), MXU_DTYPE)],
        compiler_params=_params("parallel"),
        name="conv_ffn",
    )(x2d, x2d, x2d, w_in, conv_w, conv_b, w_out, ln_g, ln_b)

def kernel(x, mem, emb_ln_g, emb_ln_b, w_in, lam_q1, lam_k1, lam_q2, lam_k2, subln_g, rpb, sc_conv_w, w_branch,
           w_mix_out, xa_q, xa_kv, xa_o, ffn_w_in, ffn_conv_w, ffn_conv_b, ffn_w_out, ln_g, ln_b):
    batch, seq, d = x.shape
    depth = w_in.shape[0]
    mem_len = mem.shape[1]
    assert seq % GRID_W == 0 and seq // GRID_W >= NA_WIN_ROWS
    alpha = (2.0 * depth) ** 0.25
    tm = min(512, seq)
    tq = min(512, seq)
    tk = min(1024, seq // 2)
    wcast = lambda a: a.astype(MXU_DTYPE)

    rope_tab = _rope_table(seq)
    bias_tab = _na_bias_table(rpb.reshape((depth * NA_HEADS,) + rpb.shape[2:]))
    stack3 = lambda a: a.reshape(depth, 1, a.shape[-1])
    lq1, lk1, lq2, lk2, sub_g, conv_b = (stack3(a) for a in (lam_q1, lam_k1, lam_q2, lam_k2, subln_g, ffn_conv_b))
    h = x.reshape(batch * seq, d)
    mem2d = mem.reshape(batch * mem_len, d)
    for l in range(depth):
        lam_init = 0.8 - 0.6 * math.exp(-0.3 * l)
        if l == 0:
            h, *proj = _input_proj(h, wcast(w_in[l]), rope_tab, seq, tm, emb_ln=(emb_ln_g, emb_ln_b))
        else:
            proj = _input_proj(h, wcast(w_in[l]), rope_tab, seq, tm)
        qa, ka, va, qb, kb, vb, gb, u, gates = proj
        ya = _diff_attention(qa, ka, va, lq1, lk1, lq2, lk2, sub_g, l, lam_init, batch, seq, tq, tk)
        yb = _neighbourhood_attention(qb, kb, vb, bias_tab, l, batch, seq, NA_ROWS_PER_STEP)
        kv = _mem_kv(mem2d, wcast(xa_kv[l]))
        h = _merge_xattn(h, ya, yb, gb, u, gates, sc_conv_w, wcast(w_branch[l]), wcast(w_mix_out[l]),
                         kv, wcast(xa_q[l]), wcast(xa_o[l]), ln_g, ln_b, l, seq, mem_len, tm, alpha)
        h = _conv_ffn(h, wcast(ffn_w_in[l]), ffn_conv_w, conv_b, wcast(ffn_w_out[l]), ln_g, ln_b,
                      l, seq, tm, alpha)
    return h.reshape(batch, seq, d)
```

```python
import functools
import math

import numpy as np
import jax
import jax.numpy as jnp
from jax import lax
from jax.experimental import pallas as pl
from jax.experimental.pallas import tpu as pltpu

GRID_W = 64
HEAD_DIM = 64
DA_HEADS = 4
DA_VDIM = 2 * HEAD_DIM
NA_HEADS = 8
NA_WIN_ROWS = 8
NA_WIN_COLS = 16
SC_W = 512
N_BRANCH = 3
XA_HEADS = 4
ROPE_THETA = 10000.0
LN_EPS = 1e-5

LANES = 128
MXU_DTYPE = jnp.bfloat16
ACT_DTYPE = jnp.bfloat16
NEG_BIG = -1e30
VMEM_LIMIT = 56 * 1024 * 1024
FFN_CHUNK = 256
NA_ROWS_PER_STEP = 16


def _params(*sem):
    return pltpu.CompilerParams(dimension_semantics=sem, vmem_limit_bytes=VMEM_LIMIT)


def _resident(shape, index_map):
    return pl.BlockSpec(shape, index_map, pipeline_mode=pl.Buffered(1))


def _layer_norm(v, g, b):
    mu = jnp.mean(v, axis=-1, keepdims=True)
    d = v - mu
    var = jnp.mean(d * d, axis=-1, keepdims=True)
    return d * lax.rsqrt(var + LN_EPS) * g + b


def _dot(a, b):
    return jnp.dot(a.astype(MXU_DTYPE), b.astype(MXU_DTYPE), preferred_element_type=jnp.float32)


def _dot_nt(a, b):
    return lax.dot_general(a.astype(MXU_DTYPE), b.astype(MXU_DTYPE), (((1,), (1,)), ((), ())),
                           preferred_element_type=jnp.float32)


def _proj_kernel(*refs, d_gate, embed_ln):
    if embed_ln:
        x_ref, g_ref, b_ref, w_ref, rope_ref, xn_ref, *outs = refs
        x = _layer_norm(x_ref[...], g_ref[...], b_ref[...])
        xn_ref[...] = x
    else:
        x_ref, w_ref, rope_ref, *outs = refs
        x = x_ref[...]
    qa_ref, ka_ref, va_ref, qb_ref, kb_ref, vb_ref, gb_ref, u_ref, gate_ref = outs
    xb = x.astype(MXU_DTYPE)
    tm = xb.shape[0]
    w = SC_W

    def mm(c0, c1):
        return jnp.dot(xb, w_ref[:, c0:c1], preferred_element_type=jnp.float32)

    lane = lax.broadcasted_iota(jnp.int32, (tm, LANES), 1)
    first_half = (lane % HEAD_DIM) < (HEAD_DIM // 2)

    def rope(y, cos, sin):
        outs = []
        for c in range(y.shape[1] // LANES):
            yc = y[:, c * LANES:(c + 1) * LANES]
            rot = jnp.where(first_half, pltpu.roll(yc, LANES - HEAD_DIM // 2, 1),
                            pltpu.roll(yc, HEAD_DIM // 2, 1))
            outs.append(yc * cos + rot * sin)
        return jnp.concatenate(outs, axis=1)

    tab = rope_ref[...]
    qa_ref[...] = rope(mm(0, w), tab[:, 0:LANES], tab[:, LANES:2 * LANES]).astype(qa_ref.dtype)
    ka_ref[...] = rope(mm(w, 2 * w), tab[:, 2 * LANES:3 * LANES], tab[:, 3 * LANES:4 * LANES]).astype(ka_ref.dtype)
    va_ref[...] = mm(2 * w, 3 * w).astype(va_ref.dtype)
    qb_ref[...] = (mm(3 * w, 4 * w) * (HEAD_DIM ** -0.5)).astype(qb_ref.dtype)
    kb_ref[...] = mm(4 * w, 5 * w).astype(kb_ref.dtype)
    vb_ref[...] = mm(5 * w, 6 * w).astype(vb_ref.dtype)
    gb_ref[...] = mm(6 * w, 7 * w).astype(gb_ref.dtype)
    u_ref[...] = (mm(7 * w, 8 * w) * mm(8 * w, 9 * w)).astype(u_ref.dtype)
    gate_ref[...] = jax.nn.sigmoid(mm(9 * w, 9 * w + d_gate)).astype(gate_ref.dtype)


def _input_proj(x2d, w_in, rope_tab, seq, tm, emb_ln=None):
    t, d = x2d.shape
    n_cols = w_in.shape[1]
    d_gate = n_cols - 9 * SC_W
    s_blocks = seq // tm
    row = lambda i: (i, 0)
    const2 = lambda i: (0, 0)
    small = jax.ShapeDtypeStruct((t, SC_W), ACT_DTYPE)
    ln_args, ln_specs, ln_shape, ln_out = [], [], [], []
    if emb_ln is not None:
        ln_args = [a.reshape(1, d) for a in emb_ln]
        ln_specs = [pl.BlockSpec((1, d), const2)] * 2
        ln_shape = [jax.ShapeDtypeStruct((t, d), jnp.float32)]
        ln_out = [pl.BlockSpec((tm, d), row)]
    return pl.pallas_call(
        functools.partial(_proj_kernel, d_gate=d_gate, embed_ln=emb_ln is not None),
        out_shape=ln_shape + [small] * 8 + [jax.ShapeDtypeStruct((t, d_gate), ACT_DTYPE)],
        grid=(t // tm,),
        in_specs=[pl.BlockSpec((tm, d), row)] + ln_specs + [
            _resident((d, n_cols), const2),
            pl.BlockSpec((tm, 4 * LANES), lambda i: (i % s_blocks, 0))],
        out_specs=ln_out + [pl.BlockSpec((tm, SC_W), row)] * 8 + [pl.BlockSpec((tm, d_gate), row)],
        compiler_params=_params("parallel"),
        name="input_proj",
    )(x2d, *ln_args, w_in, rope_tab)


def _rope_table(seq):
    half = HEAD_DIM // 2
    inv = ROPE_THETA ** (-jnp.arange(half, dtype=jnp.float32) * 2.0 / HEAD_DIM)
    ang = jnp.arange(seq, dtype=jnp.float32)[:, None] * inv[None, :]
    cos, sin = jnp.cos(ang), jnp.sin(ang)
    cos_l = jnp.tile(jnp.concatenate([cos, cos], axis=1), (1, LANES // HEAD_DIM))
    sin_l = jnp.tile(jnp.concatenate([-sin, sin], axis=1), (1, LANES // HEAD_DIM))
    scale = HEAD_DIM ** -0.5
    return jnp.concatenate([cos_l * scale, sin_l * scale, cos_l, sin_l], axis=1)


def _diff_attn_kernel(q_ref, k_ref, v_ref, lq1_ref, lk1_ref, lq2_ref, lk2_ref, g_ref, o_ref,
                      qs_ref, vt_ref, m_ref, acc_ref, *buf_refs, tq, tk, qc, n_kv, n_q, ahead, lam_init):
    seq = k_ref.shape[0]
    sub = m_ref.shape[0]
    lane = lax.broadcasted_iota(jnp.int32, (tq, LANES), 1)
    lam = (jnp.exp(jnp.sum(lq1_ref[0] * lk1_ref[0], axis=1, keepdims=True))
           - jnp.exp(jnp.sum(lq2_ref[0] * lk2_ref[0], axis=1, keepdims=True)) + lam_init)
    n_buf = len(buf_refs) // 2
    s_bufs = tuple(zip(buf_refs[:n_buf], buf_refs[n_buf:]))

    for j in range(n_kv):
        vt_ref[0:LANES, j * tk:(j + 1) * tk] = v_ref[j * tk:(j + 1) * tk, :].astype(jnp.float32).T.astype(vt_ref.dtype)
    extra = vt_ref.shape[0] - LANES
    vt_ref[LANES:, :] = jnp.where(lax.broadcasted_iota(jnp.int32, (extra, seq), 0) == 0, 1.0, 0.0).astype(vt_ref.dtype)

    def stack_queries(t):
        q = q_ref[pl.ds(pl.multiple_of(t * tq, tq), tq), :]
        zero = jnp.zeros_like(q)
        qs_ref[0:tq, :] = jnp.where(lane < HEAD_DIM, q, zero)
        qs_ref[tq:2 * tq, :] = jnp.where(lane >= HEAD_DIM, q, zero)

    def scores(j):
        s_ref, mx_ref = s_bufs[j % n_buf]
        s = _dot_nt(k_ref[j * tk:(j + 1) * tk, :], qs_ref[...])
        s_ref[...] = s
        mx_ref[...] = functools.reduce(jnp.maximum, [s[r * sub:(r + 1) * sub, :] for r in range(tk // sub)])

    def update(j):
        s_ref, mx_ref = s_bufs[j % n_buf]
        vt = vt_ref[:, j * tk:(j + 1) * tk]
        for c in range(2 * tq // qc):
            cols = slice(c * qc, (c + 1) * qc)
            m_prev = m_ref[:, cols]
            m_new = jnp.maximum(m_prev, jnp.max(mx_ref[:, cols], axis=0, keepdims=True))
            p = jnp.exp(s_ref[:, cols] - m_new[0:1, :])
            alpha = jnp.exp(m_prev - m_new)
            pv = _dot(vt, p)
            acc_ref[:, cols] = acc_ref[:, cols] * alpha[0:1, :] + pv
            m_ref[:, cols] = m_new

    def finalize(t):
        acc = acc_ref[...]
        o = acc[0:LANES, :] / acc[LANES:LANES + 1, :]
        od = o[:, 0:tq] - lam * o[:, tq:2 * tq]
        ms = jnp.mean(od * od, axis=0, keepdims=True)
        y = (od * lax.rsqrt(ms + LN_EPS)).T * g_ref[0] * (1.0 - lam_init)
        o_ref[pl.ds(pl.multiple_of(t * tq, tq), tq), :] = y.astype(o_ref.dtype)

    stack_queries(0)
    for j in range(ahead):
        scores(j)

    def tile(t, carry):
        m_ref[...] = jnp.full(m_ref.shape, NEG_BIG, jnp.float32)
        acc_ref[...] = jnp.zeros(acc_ref.shape, jnp.float32)
        for j in range(n_kv):
            if j + ahead == n_kv:
                stack_queries(jnp.minimum(t + 1, n_q - 1))
            scores((j + ahead) % n_kv)
            update(j)
        finalize(t)
        return carry

    lax.fori_loop(0, n_q, tile, 0)


def _diff_attention(qa, ka, va, lq1, lk1, lq2, lk2, subln_g, layer, lam_init, batch, seq, tq, tk):
    t = qa.shape[0]
    n_q = seq // tq
    n_kv = seq // tk
    qc = 256
    ahead = 1
    n_buf = 2 * ahead
    sub = 8
    vt_rows = LANES + 16
    assert n_kv % n_buf == 0 and (2 * tq) % qc == 0
    vspec = lambda n: pl.BlockSpec((1, 1, n), lambda b, h: (layer, 0, 0))
    head = pl.BlockSpec((seq, LANES), lambda b, h: (b, h))
    return pl.pallas_call(
        functools.partial(_diff_attn_kernel, tq=tq, tk=tk, qc=qc, n_kv=n_kv, n_q=n_q, ahead=ahead,
                          lam_init=lam_init),
        out_shape=jax.ShapeDtypeStruct((t, DA_HEADS * DA_VDIM), ACT_DTYPE),
        grid=(batch, DA_HEADS),
        in_specs=[head, head, head,
                  vspec(HEAD_DIM), vspec(HEAD_DIM), vspec(HEAD_DIM), vspec(HEAD_DIM), vspec(DA_VDIM)],
        out_specs=head,
        scratch_shapes=([pltpu.VMEM((2 * tq, LANES), qa.dtype),
                         pltpu.VMEM((vt_rows, seq), MXU_DTYPE),
                         pltpu.VMEM((sub, 2 * tq), jnp.float32),
                         pltpu.VMEM((vt_rows, 2 * tq), jnp.float32)]
                        + [pltpu.VMEM((tk, 2 * tq), jnp.float32)] * n_buf
                        + [pltpu.VMEM((sub, 2 * tq), jnp.float32)] * n_buf),
        compiler_params=_params("parallel", "parallel"),
        name="diff_attention",
    )(qa, ka, va, lq1, lk1, lq2, lk2, subln_g)


def _na_bias_table(rpb):
    nh, n_dr, n_dc = rpb.shape
    qc = np.arange(GRID_W)
    cs = np.clip(qc - NA_WIN_COLS // 2, 0, GRID_W - NA_WIN_COLS)
    kc = np.arange(GRID_W)
    valid = (kc[None, :] >= cs[:, None]) & (kc[None, :] < cs[:, None] + NA_WIN_COLS)
    p = GRID_W + NA_WIN_COLS
    row = jnp.pad(rpb.astype(jnp.float32), ((0, 0), (0, 0), (0, p - n_dc)))
    toe = jnp.tile(row, (1, 1, GRID_W))[:, :, :GRID_W * (p - 1)].reshape(nh, n_dr, GRID_W, p - 1)
    toe = toe[:, :, :, NA_WIN_COLS - 1:NA_WIN_COLS - 1 + GRID_W]
    toe = jnp.where(jnp.asarray(valid)[None, None], toe, NEG_BIG)
    hp = LANES // HEAD_DIM
    tab = toe.reshape(nh // hp, hp, n_dr, GRID_W, GRID_W).transpose(0, 2, 4, 1, 3)
    return tab.reshape(nh // hp, n_dr * GRID_W, hp * GRID_W)


def _na_kernel(q_ref, k_ref, v_ref, bias_ref, o_ref, *, rows, rows_per_step):
    i = pl.program_id(2)
    win = NA_WIN_ROWS * GRID_W
    lane = lax.broadcasted_iota(jnp.int32, (GRID_W, LANES), 1)
    lo = lane < HEAD_DIM
    for rr in range(rows_per_step):
        r = i * rows_per_step + rr
        rs = jnp.clip(r - NA_WIN_ROWS // 2, 0, rows - NA_WIN_ROWS)
        off = pl.multiple_of(rs * GRID_W, GRID_W)
        bias_off = pl.multiple_of((rs - r + NA_WIN_ROWS - 1) * GRID_W, GRID_W)
        kwin = k_ref[pl.ds(off, win), :]
        vwin = v_ref[pl.ds(off, win), :]
        q = q_ref[rr * GRID_W:(rr + 1) * GRID_W, :]
        zero = jnp.zeros_like(q)
        qst = jnp.concatenate([jnp.where(lo, q, zero), jnp.where(lo, zero, q)], axis=0)
        st = _dot_nt(kwin, qst) + bias_ref[0, pl.ds(bias_off, win), :]
        m = jnp.max(st, axis=0, keepdims=True)
        p = jnp.exp(st - m)
        l = jnp.sum(p, axis=0, keepdims=True)
        pn = (p * (1.0 / l)).astype(MXU_DTYPE)
        pv = lax.dot_general(pn, vwin.astype(MXU_DTYPE), (((0,), (0,)), ((), ())),
                             preferred_element_type=jnp.float32)
        o = jnp.where(lo, pv[0:GRID_W], pv[GRID_W:2 * GRID_W])
        o_ref[rr * GRID_W:(rr + 1) * GRID_W, :] = o.astype(o_ref.dtype)


def _neighbourhood_attention(qb, kb, vb, bias_tab, layer, batch, seq, rows_per_step):
    t = qb.shape[0]
    rows = seq // GRID_W
    tq = rows_per_step * GRID_W
    n_q = seq // tq
    pairs = NA_HEADS * HEAD_DIM // LANES
    return pl.pallas_call(
        functools.partial(_na_kernel, rows=rows, rows_per_step=rows_per_step),
        out_shape=jax.ShapeDtypeStruct((t, NA_HEADS * HEAD_DIM), ACT_DTYPE),
        grid=(batch, pairs, n_q),
        in_specs=[pl.BlockSpec((tq, LANES), lambda b, h, i: (b * n_q + i, h)),
                  pl.BlockSpec((seq, LANES), lambda b, h, i: (b, h)),
                  pl.BlockSpec((seq, LANES), lambda b, h, i: (b, h)),
                  pl.BlockSpec((1,) + bias_tab.shape[1:], lambda b, h, i: (layer * pairs + h, 0, 0))],
        out_specs=pl.BlockSpec((tq, LANES), lambda b, h, i: (b * n_q + i, h)),
        compiler_params=_params("parallel", "parallel", "parallel"),
        name="neighbourhood_attention",
    )(qb, kb, vb, bias_tab)


def _shift_rows(a, prev_row, next_row):
    n = a.shape[0]
    ridx = lax.broadcasted_iota(jnp.int32, a.shape, 0)
    up = jnp.where(ridx == 0, prev_row, pltpu.roll(a, 1, 0))
    dn = jnp.where(ridx == n - 1, next_row, pltpu.roll(a, n - 1, 0))
    return up, dn


def _merge_xattn_kernel(x_ref, ya_ref, yb_ref, gb_ref, u_ref, up_ref, un_ref, gate_ref, cw_ref, wb_ref, wm_ref,
                        k_ref, v_ref, wq_ref, wo_ref, g_ref, b_ref, o_ref,
                        *, s_blocks, halo, alpha, d_model, n_sub):
    i = pl.program_id(0)
    has_prev = i % s_blocks != 0
    has_next = i % s_blocks != s_blocks - 1
    u = u_ref[...].astype(jnp.float32)
    prev_row = jnp.where(has_prev, up_ref[halo - 1:halo, :].astype(jnp.float32), 0.0)
    next_row = jnp.where(has_next, un_ref[0:1, :].astype(jnp.float32), 0.0)
    u_up, u_dn = _shift_rows(u, prev_row, next_row)
    cw = cw_ref[0]
    conv = u_up * cw[0:1, :] + u * cw[1:2, :] + u_dn * cw[2:3, :]
    yc = (gb_ref[...].astype(jnp.float32) * conv).astype(MXU_DTYPE)
    ln_g, ln_b = g_ref[0], b_ref[0]
    k = k_ref[...]
    v = v_ref[...]
    hd = d_model // XA_HEADS
    ts = x_ref.shape[0] // n_sub
    subs = [slice(sub * ts, (sub + 1) * ts) for sub in range(n_sub)]

    def merge(rows):
        gate = gate_ref[rows, :].astype(jnp.float32)
        merged = (gate[:, 0:d_model] * _dot(ya_ref[rows, :], wb_ref[0])
                  + gate[:, d_model:2 * d_model] * _dot(yb_ref[rows, :], wb_ref[1])
                  + gate[:, 2 * d_model:3 * d_model] * _dot(yc[rows, :], wb_ref[2]))
        return alpha * x_ref[rows, :] + _dot(merged, wm_ref[...])

    def attend(q):
        outs = []
        for h in range(XA_HEADS):
            sl = slice(h * hd, (h + 1) * hd)
            s = _dot_nt(q[:, sl], k[:, sl])
            m = jnp.max(s, axis=1, keepdims=True)
            p = jnp.exp(s - m)
            l = jnp.sum(p, axis=1, keepdims=True)
            outs.append(_dot(p, v[:, sl]) / l)
        return jnp.concatenate(outs, axis=1)

    pre = [merge(rows) for rows in subs]
    x1 = [_layer_norm(a, ln_g[0:1, :], ln_b[0:1, :]) for a in pre]
    q = [_dot(a, wq_ref[...]) * (hd ** -0.5) for a in x1]
    att = [attend(a) for a in q]
    h_out = [_dot(a, wo_ref[...]) for a in att]
    for rows, a, b in zip(subs, x1, h_out):
        o_ref[rows, :] = _layer_norm(alpha * a + b, ln_g[1:2, :], ln_b[1:2, :])


def _merge_xattn(x2d, ya, yb, gb, u, gates, conv_w, w_branch, w_mix, kv, xa_q, xa_o, ln_g, ln_b,
                 layer, seq, mem_len, tm, alpha):
    t, d = x2d.shape
    halo = 16
    hb = tm // halo
    n_halo = t // halo
    s_blocks = seq // tm
    row = lambda i: (i, 0)
    const2 = lambda i: (0, 0)
    per_layer = lambda a: pl.BlockSpec((1,) + a.shape[1:], lambda i: (layer,) + (0,) * (a.ndim - 1))
    return pl.pallas_call(
        functools.partial(_merge_xattn_kernel, s_blocks=s_blocks, halo=halo, alpha=alpha, d_model=d,
                          n_sub=2 if tm % 32 == 0 else 1),
        out_shape=jax.ShapeDtypeStruct((t, d), jnp.float32),
        grid=(t // tm,),
        in_specs=[pl.BlockSpec((tm, d), row),
                  pl.BlockSpec((tm, SC_W), row),
                  pl.BlockSpec((tm, SC_W), row),
                  pl.BlockSpec((tm, SC_W), row),
                  pl.BlockSpec((tm, SC_W), row),
                  pl.BlockSpec((halo, SC_W), lambda i: (jnp.maximum(i * hb - 1, 0), 0)),
                  pl.BlockSpec((halo, SC_W), lambda i: (jnp.minimum((i + 1) * hb, n_halo - 1), 0)),
                  pl.BlockSpec((tm, N_BRANCH * d), row),
                  per_layer(conv_w),
                  _resident(w_branch.shape, lambda i: (0, 0, 0)),
                  _resident(w_mix.shape, const2),
                  pl.BlockSpec((mem_len, d), lambda i: (i // s_blocks, 0)),
                  pl.BlockSpec((mem_len, d), lambda i: (i // s_blocks, 1)),
                  _resident((d, d), const2),
                  _resident((d, d), const2),
                  per_layer(ln_g),
                  per_layer(ln_b)],
        out_specs=pl.BlockSpec((tm, d), row),
        compiler_params=_params("parallel"),
        name="merge_xattn",
    )(x2d, ya, yb, gb, u, u, u, gates, conv_w, w_branch, w_mix, kv, kv, xa_q, xa_o, ln_g, ln_b)


def _kv_kernel(m_ref, w_ref, o_ref):
    o_ref[...] = _dot(m_ref[...], w_ref[...]).astype(o_ref.dtype)


def _mem_kv(mem2d, xa_kv):
    t, d = mem2d.shape
    n = xa_kv.shape[1]
    tm = min(t, 256)
    return pl.pallas_call(
        _kv_kernel,
        out_shape=jax.ShapeDtypeStruct((t, n), ACT_DTYPE),
        grid=(t // tm,),
        in_specs=[pl.BlockSpec((tm, d), lambda i: (i, 0)), _resident((d, n), lambda i: (0, 0))],
        out_specs=pl.BlockSpec((tm, n), lambda i: (i, 0)),
        compiler_params=_params("parallel"),
        name="mem_kv",
    )(mem2d, xa_kv)


def _ffn_kernel(x_ref, xp_ref, xn_ref, wi_ref, cw_ref, cb_ref, wo_ref, g_ref, b_ref, o_ref, hid_ref,
                *, s_blocks, halo, alpha, d_ff, tf):
    i = pl.program_id(0)
    has_prev = i % s_blocks != 0
    has_next = i % s_blocks != s_blocks - 1
    x = x_ref[...]
    xb = x.astype(MXU_DTYPE)
    cw_all = cw_ref[0]
    ln_g, ln_b = g_ref[0], b_ref[0]
    xe = jnp.concatenate([jnp.where(has_prev, xp_ref[halo - 1:halo, :], 0.0),
                          jnp.where(has_next, xn_ref[0:1, :], 0.0),
                          jnp.zeros((halo - 2, x.shape[1]), jnp.float32)], axis=0).astype(MXU_DTYPE)
    for c in range(d_ff // tf):
        cols = slice(c * tf, (c + 1) * tf)
        w_c = jnp.concatenate([wi_ref[:, cols], wi_ref[:, d_ff + c * tf:d_ff + (c + 1) * tf]], axis=1)
        ug = jnp.dot(xb, w_c, preferred_element_type=jnp.float32)
        u, gt = ug[:, 0:tf], ug[:, tf:2 * tf]
        ge = jnp.dot(xe, w_c, preferred_element_type=jnp.float32)[:, tf:2 * tf]
        g_up, g_dn = _shift_rows(gt, ge[0:1, :], ge[1:2, :])
        cw = cw_all[:, cols]
        a = g_up * cw[0:1, :] + gt * cw[1:2, :] + g_dn * cw[2:3, :] + cb_ref[0, :, cols]
        hid_ref[:, cols] = ((a * jax.nn.sigmoid(a)) * u).astype(hid_ref.dtype)
    h = jnp.dot(hid_ref[...], wo_ref[...], preferred_element_type=jnp.float32)
    o_ref[...] = _layer_norm(alpha * x + h, ln_g[2:3, :], ln_b[2:3, :])


def _conv_ffn(x2d, w_in, conv_w, conv_b, w_out, ln_g, ln_b, layer, seq, tm, alpha):
    t, d = x2d.shape
    d_ff = w_out.shape[0]
    tf = FFN_CHUNK
    halo = 8
    hb = tm // halo
    n_halo = t // halo
    row = lambda i: (i, 0)
    const2 = lambda i: (0, 0)
    per_layer = lambda a: pl.BlockSpec((1,) + a.shape[1:], lambda i: (layer,) + (0,) * (a.ndim - 1))
    return pl.pallas_call(
        functools.partial(_ffn_kernel, s_blocks=seq // tm, halo=halo, alpha=alpha, d_ff=d_ff, tf=tf),
        out_shape=jax.ShapeDtypeStruct((t, d), jnp.float32),
        grid=(t // tm,),
        in_specs=[pl.BlockSpec((tm, d), row),
                  pl.BlockSpec((halo, d), lambda i: (jnp.maximum(i * hb - 1, 0), 0)),
                  pl.BlockSpec((halo, d), lambda i: (jnp.minimum((i + 1) * hb, n_halo - 1), 0)),
                  _resident(w_in.shape, const2),
                  per_layer(conv_w),
                  per_layer(conv_b),
                  _resident(w_out.shape, const2),
                  per_layer(ln_g),
                  per_layer(ln_b)],
        out_specs=pl.BlockSpec((tm, d), row),
        scratch_shapes=[pltpu.VMEM((tm, d_ff), MXU_DTYPE)],
        compiler_params=_params("parallel"),
        name="conv_ffn",
    )(x2d, x2d, x2d, w_in, conv_w, conv_b, w_out, ln_g, ln_b)

def kernel(x, mem, emb_ln_g, emb_ln_b, w_in, lam_q1, lam_k1, lam_q2, lam_k2, subln_g, rpb, sc_conv_w, w_branch,
           w_mix_out, xa_q, xa_kv, xa_o, ffn_w_in, ffn_conv_w, ffn_conv_b, ffn_w_out, ln_g, ln_b):
    batch, seq, d = x.shape
    depth = w_in.shape[0]
    mem_len = mem.shape[1]
    assert seq % GRID_W == 0 and seq // GRID_W >= NA_WIN_ROWS
    alpha = (2.0 * depth) ** 0.25
    tm = min(512, seq)
    tm_ffn = min(1024, seq)
    tq = min(512, seq)
    tk = min(1024, seq // 2)
    wcast = lambda a: a.astype(MXU_DTYPE)

    rope_tab = _rope_table(seq)
    bias_tab = _na_bias_table(rpb.reshape((depth * NA_HEADS,) + rpb.shape[2:]))
    stack3 = lambda a: a.reshape(depth, 1, a.shape[-1])
    lq1, lk1, lq2, lk2, sub_g, conv_b = (stack3(a) for a in (lam_q1, lam_k1, lam_q2, lam_k2, subln_g, ffn_conv_b))
    h = x.reshape(batch * seq, d)
    mem2d = mem.reshape(batch * mem_len, d)
    for l in range(depth):
        lam_init = 0.8 - 0.6 * math.exp(-0.3 * l)
        if l == 0:
            h, *proj = _input_proj(h, wcast(w_in[l]), rope_tab, seq, tm, emb_ln=(emb_ln_g, emb_ln_b))
        else:
            proj = _input_proj(h, wcast(w_in[l]), rope_tab, seq, tm)
        qa, ka, va, qb, kb, vb, gb, u, gates = proj
        ya = _diff_attention(qa, ka, va, lq1, lk1, lq2, lk2, sub_g, l, lam_init, batch, seq, tq, tk)
        yb = _neighbourhood_attention(qb, kb, vb, bias_tab, l, batch, seq, NA_ROWS_PER_STEP)
        kv = _mem_kv(mem2d, wcast(xa_kv[l]))
        h = _merge_xattn(h, ya, yb, gb, u, gates, sc_conv_w, wcast(w_branch[l]), wcast(w_mix_out[l]),
                         kv, wcast(xa_q[l]), wcast(xa_o[l]), ln_g, ln_b, l, seq, mem_len, tm, alpha)
        h = _conv_ffn(h, wcast(ffn_w_in[l]), ffn_conv_w, conv_b, wcast(ffn_w_out[l]), ln_g, ln_b,
                      l, seq, tm_ffn, alpha)
    return h.reshape(batch, seq, d)
```

```python
import functools
import math

import numpy as np
import jax
import jax.numpy as jnp
from jax import lax
from jax.experimental import pallas as pl
from jax.experimental.pallas import tpu as pltpu

GRID_W = 64
HEAD_DIM = 64
DA_HEADS = 4
DA_VDIM = 2 * HEAD_DIM
NA_HEADS = 8
NA_WIN_ROWS = 8
NA_WIN_COLS = 16
SC_W = 512
N_BRANCH = 3
XA_HEADS = 4
ROPE_THETA = 10000.0
LN_EPS = 1e-5

LANES = 128
MXU_DTYPE = jnp.bfloat16
ACT_DTYPE = jnp.bfloat16
NEG_BIG = -1e30
VMEM_LIMIT = 56 * 1024 * 1024
FFN_CHUNK = 256
NA_ROWS_PER_STEP = 16


def _params(*sem):
    return pltpu.CompilerParams(dimension_semantics=sem, vmem_limit_bytes=VMEM_LIMIT)


def _resident(shape, index_map):
    return pl.BlockSpec(shape, index_map, pipeline_mode=pl.Buffered(1))


def _layer_norm(v, g, b):
    mu = jnp.mean(v, axis=-1, keepdims=True)
    d = v - mu
    var = jnp.mean(d * d, axis=-1, keepdims=True)
    return d * lax.rsqrt(var + LN_EPS) * g + b


def _dot(a, b):
    return jnp.dot(a.astype(MXU_DTYPE), b.astype(MXU_DTYPE), preferred_element_type=jnp.float32)


def _dot_nt(a, b):
    return lax.dot_general(a.astype(MXU_DTYPE), b.astype(MXU_DTYPE), (((1,), (1,)), ((), ())),
                           preferred_element_type=jnp.float32)


def _proj_kernel(*refs, d_gate, embed_ln):
    if embed_ln:
        x_ref, g_ref, b_ref, w_ref, rope_ref, xn_ref, *outs = refs
        x = _layer_norm(x_ref[...], g_ref[...], b_ref[...])
        xn_ref[...] = x
    else:
        x_ref, w_ref, rope_ref, *outs = refs
        x = x_ref[...]
    qa_ref, ka_ref, va_ref, qb_ref, kb_ref, vb_ref, gb_ref, u_ref, gate_ref = outs
    xb = x.astype(MXU_DTYPE)
    tm = xb.shape[0]
    w = SC_W

    def mm(c0, c1):
        return jnp.dot(xb, w_ref[:, c0:c1], preferred_element_type=jnp.float32)

    lane = lax.broadcasted_iota(jnp.int32, (tm, LANES), 1)
    first_half = (lane % HEAD_DIM) < (HEAD_DIM // 2)

    def rope(y, cos, sin):
        outs = []
        for c in range(y.shape[1] // LANES):
            yc = y[:, c * LANES:(c + 1) * LANES]
            rot = jnp.where(first_half, pltpu.roll(yc, LANES - HEAD_DIM // 2, 1),
                            pltpu.roll(yc, HEAD_DIM // 2, 1))
            outs.append(yc * cos + rot * sin)
        return jnp.concatenate(outs, axis=1)

    tab = rope_ref[...]
    qa_ref[...] = rope(mm(0, w), tab[:, 0:LANES], tab[:, LANES:2 * LANES]).astype(qa_ref.dtype)
    ka_ref[...] = rope(mm(w, 2 * w), tab[:, 2 * LANES:3 * LANES], tab[:, 3 * LANES:4 * LANES]).astype(ka_ref.dtype)
    va_ref[...] = mm(2 * w, 3 * w).astype(va_ref.dtype)
    qb_ref[...] = (mm(3 * w, 4 * w) * (HEAD_DIM ** -0.5)).astype(qb_ref.dtype)
    kb_ref[...] = mm(4 * w, 5 * w).astype(kb_ref.dtype)
    vb_ref[...] = mm(5 * w, 6 * w).astype(vb_ref.dtype)
    gb_ref[...] = mm(6 * w, 7 * w).astype(gb_ref.dtype)
    u_ref[...] = (mm(7 * w, 8 * w) * mm(8 * w, 9 * w)).astype(u_ref.dtype)
    gate_ref[...] = jax.nn.sigmoid(mm(9 * w, 9 * w + d_gate)).astype(gate_ref.dtype)


def _input_proj(x2d, w_in, rope_tab, seq, tm, emb_ln=None):
    t, d = x2d.shape
    n_cols = w_in.shape[1]
    d_gate = n_cols - 9 * SC_W
    s_blocks = seq // tm
    row = lambda i: (i, 0)
    const2 = lambda i: (0, 0)
    small = jax.ShapeDtypeStruct((t, SC_W), ACT_DTYPE)
    ln_args, ln_specs, ln_shape, ln_out = [], [], [], []
    if emb_ln is not None:
        ln_args = [a.reshape(1, d) for a in emb_ln]
        ln_specs = [pl.BlockSpec((1, d), const2)] * 2
        ln_shape = [jax.ShapeDtypeStruct((t, d), jnp.float32)]
        ln_out = [pl.BlockSpec((tm, d), row)]
    return pl.pallas_call(
        functools.partial(_proj_kernel, d_gate=d_gate, embed_ln=emb_ln is not None),
        out_shape=ln_shape + [small] * 8 + [jax.ShapeDtypeStruct((t, d_gate), ACT_DTYPE)],
        grid=(t // tm,),
        in_specs=[pl.BlockSpec((tm, d), row)] + ln_specs + [
            _resident((d, n_cols), const2),
            pl.BlockSpec((tm, 4 * LANES), lambda i: (i % s_blocks, 0))],
        out_specs=ln_out + [pl.BlockSpec((tm, SC_W), row)] * 8 + [pl.BlockSpec((tm, d_gate), row)],
        compiler_params=_params("parallel"),
        name="input_proj",
    )(x2d, *ln_args, w_in, rope_tab)


def _rope_table(seq):
    half = HEAD_DIM // 2
    inv = ROPE_THETA ** (-jnp.arange(half, dtype=jnp.float32) * 2.0 / HEAD_DIM)
    ang = jnp.arange(seq, dtype=jnp.float32)[:, None] * inv[None, :]
    cos, sin = jnp.cos(ang), jnp.sin(ang)
    cos_l = jnp.tile(jnp.concatenate([cos, cos], axis=1), (1, LANES // HEAD_DIM))
    sin_l = jnp.tile(jnp.concatenate([-sin, sin], axis=1), (1, LANES // HEAD_DIM))
    scale = HEAD_DIM ** -0.5
    return jnp.concatenate([cos_l * scale, sin_l * scale, cos_l, sin_l], axis=1)


def _diff_attn_kernel(q_ref, k_ref, v_ref, lq1_ref, lk1_ref, lq2_ref, lk2_ref, g_ref, o_ref,
                      qs_ref, vt_ref, m_ref, acc_ref, *buf_refs, tq, tk, qc, n_kv, n_q, ahead, lam_init):
    sub = m_ref.shape[0]
    n_c = 2 * tq // qc
    lane = lax.broadcasted_iota(jnp.int32, (tq, LANES), 1)
    lam = (jnp.exp(jnp.sum(lq1_ref[0] * lk1_ref[0], axis=1, keepdims=True))
           - jnp.exp(jnp.sum(lq2_ref[0] * lk2_ref[0], axis=1, keepdims=True)) + lam_init)
    n_buf = len(buf_refs) // 2
    s_bufs = tuple(zip(buf_refs[:n_buf], buf_refs[n_buf:]))

    extra = vt_ref.shape[1] - LANES
    ones_rows = jnp.where(lax.broadcasted_iota(jnp.int32, (extra, tk), 0) == 0, 1.0, 0.0).astype(vt_ref.dtype)
    for j in range(n_kv):
        vt_ref[j, 0:LANES, :] = v_ref[j * tk:(j + 1) * tk, :].astype(jnp.float32).T.astype(vt_ref.dtype)
        vt_ref[j, LANES:, :] = ones_rows

    def stack_queries(t):
        q = q_ref[pl.ds(pl.multiple_of(t * tq, tq), tq), :]
        zero = jnp.zeros_like(q)
        qs_ref[0:tq, :] = jnp.where(lane < HEAD_DIM, q, zero)
        qs_ref[tq:2 * tq, :] = jnp.where(lane >= HEAD_DIM, q, zero)

    def scores(j):
        s_ref, mx_ref = s_bufs[j % n_buf]
        s = _dot_nt(k_ref[j * tk:(j + 1) * tk, :], qs_ref[...])
        for c in range(n_c):
            s_ref[c] = s[:, c * qc:(c + 1) * qc]
        mx_ref[...] = functools.reduce(jnp.maximum, [s[r * sub:(r + 1) * sub, :] for r in range(tk // sub)])

    def update(j):
        s_ref, mx_ref = s_bufs[j % n_buf]
        vt = vt_ref[j]
        for c in range(n_c):
            cols = slice(c * qc, (c + 1) * qc)
            m_prev = m_ref[:, cols]
            m_new = jnp.maximum(m_prev, jnp.max(mx_ref[:, cols], axis=0, keepdims=True))
            p = jnp.exp(s_ref[c] - m_new[0:1, :])
            alpha = jnp.exp(m_prev - m_new)
            pv = _dot(vt, p)
            acc_ref[c] = acc_ref[c] * alpha[0:1, :] + pv
            m_ref[:, cols] = m_new

    def finalize(t):
        acc = jnp.concatenate([acc_ref[c] for c in range(n_c)], axis=1)
        o = acc[0:LANES, :] / acc[LANES:LANES + 1, :]
        od = o[:, 0:tq] - lam * o[:, tq:2 * tq]
        ms = jnp.mean(od * od, axis=0, keepdims=True)
        y = (od * lax.rsqrt(ms + LN_EPS)).T * g_ref[0] * (1.0 - lam_init)
        o_ref[pl.ds(pl.multiple_of(t * tq, tq), tq), :] = y.astype(o_ref.dtype)

    stack_queries(0)
    for j in range(ahead):
        scores(j)

    def tile(t, carry):
        m_ref[...] = jnp.full(m_ref.shape, NEG_BIG, jnp.float32)
        acc_ref[...] = jnp.zeros(acc_ref.shape, jnp.float32)
        for j in range(n_kv):
            if j + ahead == n_kv:
                stack_queries(jnp.minimum(t + 1, n_q - 1))
            scores((j + ahead) % n_kv)
            update(j)
        finalize(t)
        return carry

    lax.fori_loop(0, n_q, tile, 0)


def _diff_attention(qa, ka, va, lq1, lk1, lq2, lk2, subln_g, layer, lam_init, batch, seq, tq, tk):
    t = qa.shape[0]
    n_q = seq // tq
    n_kv = seq // tk
    qc = 256
    ahead = 1
    n_buf = 2 * ahead
    sub = 8
    vt_rows = LANES + 16
    assert n_kv % n_buf == 0 and (2 * tq) % qc == 0
    vspec = lambda n: pl.BlockSpec((1, 1, n), lambda b, h: (layer, 0, 0))
    head = pl.BlockSpec((seq, LANES), lambda b, h: (b, h))
    return pl.pallas_call(
        functools.partial(_diff_attn_kernel, tq=tq, tk=tk, qc=qc, n_kv=n_kv, n_q=n_q, ahead=ahead,
                          lam_init=lam_init),
        out_shape=jax.ShapeDtypeStruct((t, DA_HEADS * DA_VDIM), ACT_DTYPE),
        grid=(batch, DA_HEADS),
        in_specs=[head, head, head,
                  vspec(HEAD_DIM), vspec(HEAD_DIM), vspec(HEAD_DIM), vspec(HEAD_DIM), vspec(DA_VDIM)],
        out_specs=head,
        scratch_shapes=([pltpu.VMEM((2 * tq, LANES), qa.dtype),
                         pltpu.VMEM((n_kv, vt_rows, tk), MXU_DTYPE),
                         pltpu.VMEM((sub, 2 * tq), jnp.float32),
                         pltpu.VMEM((2 * tq // qc, vt_rows, qc), jnp.float32)]
                        + [pltpu.VMEM((2 * tq // qc, tk, qc), jnp.float32)] * n_buf
                        + [pltpu.VMEM((sub, 2 * tq), jnp.float32)] * n_buf),
        compiler_params=_params("parallel", "parallel"),
        name="diff_attention",
    )(qa, ka, va, lq1, lk1, lq2, lk2, subln_g)


def _na_bias_table(rpb):
    nh, n_dr, n_dc = rpb.shape
    qc = np.arange(GRID_W)
    cs = np.clip(qc - NA_WIN_COLS // 2, 0, GRID_W - NA_WIN_COLS)
    kc = np.arange(GRID_W)
    valid = (kc[None, :] >= cs[:, None]) & (kc[None, :] < cs[:, None] + NA_WIN_COLS)
    p = GRID_W + NA_WIN_COLS
    row = jnp.pad(rpb.astype(jnp.float32), ((0, 0), (0, 0), (0, p - n_dc)))
    toe = jnp.tile(row, (1, 1, GRID_W))[:, :, :GRID_W * (p - 1)].reshape(nh, n_dr, GRID_W, p - 1)
    toe = toe[:, :, :, NA_WIN_COLS - 1:NA_WIN_COLS - 1 + GRID_W]
    toe = jnp.where(jnp.asarray(valid)[None, None], toe, NEG_BIG)
    hp = LANES // HEAD_DIM
    tab = toe.reshape(nh // hp, hp, n_dr, GRID_W, GRID_W).transpose(0, 2, 4, 1, 3)
    return tab.reshape(nh // hp, n_dr * GRID_W, hp * GRID_W)


def _na_kernel(q_ref, k_ref, v_ref, bias_ref, o_ref, *, rows, rows_per_step):
    i = pl.program_id(2)
    win = NA_WIN_ROWS * GRID_W
    lane = lax.broadcasted_iota(jnp.int32, (GRID_W, LANES), 1)
    lo = lane < HEAD_DIM
    for rr in range(rows_per_step):
        r = i * rows_per_step + rr
        rs = jnp.clip(r - NA_WIN_ROWS // 2, 0, rows - NA_WIN_ROWS)
        off = pl.multiple_of(rs * GRID_W, GRID_W)
        bias_off = pl.multiple_of((rs - r + NA_WIN_ROWS - 1) * GRID_W, GRID_W)
        kwin = k_ref[pl.ds(off, win), :]
        vwin = v_ref[pl.ds(off, win), :]
        q = q_ref[rr * GRID_W:(rr + 1) * GRID_W, :]
        zero = jnp.zeros_like(q)
        qst = jnp.concatenate([jnp.where(lo, q, zero), jnp.where(lo, zero, q)], axis=0)
        st = _dot_nt(kwin, qst) + bias_ref[0, pl.ds(bias_off, win), :]
        m = jnp.max(st, axis=0, keepdims=True)
        p = jnp.exp(st - m)
        l = jnp.sum(p, axis=0, keepdims=True)
        pn = (p * (1.0 / l)).astype(MXU_DTYPE)
        pv = lax.dot_general(pn, vwin.astype(MXU_DTYPE), (((0,), (0,)), ((), ())),
                             preferred_element_type=jnp.float32)
        o = jnp.where(lo, pv[0:GRID_W], pv[GRID_W:2 * GRID_W])
        o_ref[rr * GRID_W:(rr + 1) * GRID_W, :] = o.astype(o_ref.dtype)


def _neighbourhood_attention(qb, kb, vb, bias_tab, layer, batch, seq, rows_per_step):
    t = qb.shape[0]
    rows = seq // GRID_W
    tq = rows_per_step * GRID_W
    n_q = seq // tq
    pairs = NA_HEADS * HEAD_DIM // LANES
    return pl.pallas_call(
        functools.partial(_na_kernel, rows=rows, rows_per_step=rows_per_step),
        out_shape=jax.ShapeDtypeStruct((t, NA_HEADS * HEAD_DIM), ACT_DTYPE),
        grid=(batch, pairs, n_q),
        in_specs=[pl.BlockSpec((tq, LANES), lambda b, h, i: (b * n_q + i, h)),
                  pl.BlockSpec((seq, LANES), lambda b, h, i: (b, h)),
                  pl.BlockSpec((seq, LANES), lambda b, h, i: (b, h)),
                  pl.BlockSpec((1,) + bias_tab.shape[1:], lambda b, h, i: (layer * pairs + h, 0, 0))],
        out_specs=pl.BlockSpec((tq, LANES), lambda b, h, i: (b * n_q + i, h)),
        compiler_params=_params("parallel", "parallel", "parallel"),
        name="neighbourhood_attention",
    )(qb, kb, vb, bias_tab)


def _shift_rows(a, prev_row, next_row):
    n = a.shape[0]
    ridx = lax.broadcasted_iota(jnp.int32, a.shape, 0)
    up = jnp.where(ridx == 0, prev_row, pltpu.roll(a, 1, 0))
    dn = jnp.where(ridx == n - 1, next_row, pltpu.roll(a, n - 1, 0))
    return up, dn


def _merge_xattn_kernel(x_ref, ya_ref, yb_ref, gb_ref, u_ref, up_ref, un_ref, gate_ref, cw_ref, wb_ref, wm_ref,
                        k_ref, v_ref, wq_ref, wo_ref, g_ref, b_ref, o_ref,
                        *, s_blocks, halo, alpha, d_model, n_sub):
    i = pl.program_id(0)
    has_prev = i % s_blocks != 0
    has_next = i % s_blocks != s_blocks - 1
    u = u_ref[...].astype(jnp.float32)
    prev_row = jnp.where(has_prev, up_ref[halo - 1:halo, :].astype(jnp.float32), 0.0)
    next_row = jnp.where(has_next, un_ref[0:1, :].astype(jnp.float32), 0.0)
    u_up, u_dn = _shift_rows(u, prev_row, next_row)
    cw = cw_ref[0]
    conv = u_up * cw[0:1, :] + u * cw[1:2, :] + u_dn * cw[2:3, :]
    yc = (gb_ref[...].astype(jnp.float32) * conv).astype(MXU_DTYPE)
    ln_g, ln_b = g_ref[0], b_ref[0]
    k = k_ref[...]
    v = v_ref[...]
    hd = d_model // XA_HEADS
    ts = x_ref.shape[0] // n_sub
    subs = [slice(sub * ts, (sub + 1) * ts) for sub in range(n_sub)]

    def merge(rows):
        gate = gate_ref[rows, :].astype(jnp.float32)
        merged = (gate[:, 0:d_model] * _dot(ya_ref[rows, :], wb_ref[0])
                  + gate[:, d_model:2 * d_model] * _dot(yb_ref[rows, :], wb_ref[1])
                  + gate[:, 2 * d_model:3 * d_model] * _dot(yc[rows, :], wb_ref[2]))
        return alpha * x_ref[rows, :] + _dot(merged, wm_ref[...])

    def attend(q):
        outs = []
        for h in range(XA_HEADS):
            sl = slice(h * hd, (h + 1) * hd)
            s = _dot_nt(q[:, sl], k[:, sl])
            m = jnp.max(s, axis=1, keepdims=True)
            p = jnp.exp(s - m)
            l = jnp.sum(p, axis=1, keepdims=True)
            outs.append(_dot(p, v[:, sl]) / l)
        return jnp.concatenate(outs, axis=1)

    pre = [merge(rows) for rows in subs]
    x1 = [_layer_norm(a, ln_g[0:1, :], ln_b[0:1, :]) for a in pre]
    q = [_dot(a, wq_ref[...]) * (hd ** -0.5) for a in x1]
    att = [attend(a) for a in q]
    h_out = [_dot(a, wo_ref[...]) for a in att]
    for rows, a, b in zip(subs, x1, h_out):
        o_ref[rows, :] = _layer_norm(alpha * a + b, ln_g[1:2, :], ln_b[1:2, :])


def _merge_xattn(x2d, ya, yb, gb, u, gates, conv_w, w_branch, w_mix, kv, xa_q, xa_o, ln_g, ln_b,
                 layer, seq, mem_len, tm, alpha):
    t, d = x2d.shape
    halo = 16
    hb = tm // halo
    n_halo = t // halo
    s_blocks = seq // tm
    row = lambda i: (i, 0)
    const2 = lambda i: (0, 0)
    per_layer = lambda a: pl.BlockSpec((1,) + a.shape[1:], lambda i: (layer,) + (0,) * (a.ndim - 1))
    return pl.pallas_call(
        functools.partial(_merge_xattn_kernel, s_blocks=s_blocks, halo=halo, alpha=alpha, d_model=d,
                          n_sub=2 if tm % 32 == 0 else 1),
        out_shape=jax.ShapeDtypeStruct((t, d), jnp.float32),
        grid=(t // tm,),
        in_specs=[pl.BlockSpec((tm, d), row),
                  pl.BlockSpec((tm, SC_W), row),
                  pl.BlockSpec((tm, SC_W), row),
                  pl.BlockSpec((tm, SC_W), row),
                  pl.BlockSpec((tm, SC_W), row),
                  pl.BlockSpec((halo, SC_W), lambda i: (jnp.maximum(i * hb - 1, 0), 0)),
                  pl.BlockSpec((halo, SC_W), lambda i: (jnp.minimum((i + 1) * hb, n_halo - 1), 0)),
                  pl.BlockSpec((tm, N_BRANCH * d), row),
                  per_layer(conv_w),
                  _resident(w_branch.shape, lambda i: (0, 0, 0)),
                  _resident(w_mix.shape, const2),
                  pl.BlockSpec((mem_len, d), lambda i: (i // s_blocks, 0)),
                  pl.BlockSpec((mem_len, d), lambda i: (i // s_blocks, 1)),
                  _resident((d, d), const2),
                  _resident((d, d), const2),
                  per_layer(ln_g),
                  per_layer(ln_b)],
        out_specs=pl.BlockSpec((tm, d), row),
        compiler_params=_params("parallel"),
        name="merge_xattn",
    )(x2d, ya, yb, gb, u, u, u, gates, conv_w, w_branch, w_mix, kv, kv, xa_q, xa_o, ln_g, ln_b)


def _kv_kernel(m_ref, w_ref, o_ref):
    o_ref[...] = _dot(m_ref[...], w_ref[...]).astype(o_ref.dtype)


def _mem_kv(mem2d, xa_kv):
    t, d = mem2d.shape
    n = xa_kv.shape[1]
    tm = min(t, 256)
    return pl.pallas_call(
        _kv_kernel,
        out_shape=jax.ShapeDtypeStruct((t, n), ACT_DTYPE),
        grid=(t // tm,),
        in_specs=[pl.BlockSpec((tm, d), lambda i: (i, 0)), _resident((d, n), lambda i: (0, 0))],
        out_specs=pl.BlockSpec((tm, n), lambda i: (i, 0)),
        compiler_params=_params("parallel"),
        name="mem_kv",
    )(mem2d, xa_kv)


def _ffn_kernel(x_ref, xp_ref, xn_ref, wi_ref, cw_ref, cb_ref, wo_ref, g_ref, b_ref, o_ref, hid_ref,
                *, s_blocks, halo, alpha, d_ff, tf):
    i = pl.program_id(0)
    has_prev = i % s_blocks != 0
    has_next = i % s_blocks != s_blocks - 1
    x = x_ref[...]
    xb = x.astype(MXU_DTYPE)
    cw_all = cw_ref[0]
    ln_g, ln_b = g_ref[0], b_ref[0]
    xe = jnp.concatenate([jnp.where(has_prev, xp_ref[halo - 1:halo, :], 0.0),
                          jnp.where(has_next, xn_ref[0:1, :], 0.0),
                          jnp.zeros((halo - 2, x.shape[1]), jnp.float32)], axis=0).astype(MXU_DTYPE)
    for c in range(d_ff // tf):
        cols = slice(c * tf, (c + 1) * tf)
        w_c = jnp.concatenate([wi_ref[:, cols], wi_ref[:, d_ff + c * tf:d_ff + (c + 1) * tf]], axis=1)
        ug = jnp.dot(xb, w_c, preferred_element_type=jnp.float32)
        u, gt = ug[:, 0:tf], ug[:, tf:2 * tf]
        ge = jnp.dot(xe, w_c, preferred_element_type=jnp.float32)[:, tf:2 * tf]
        g_up, g_dn = _shift_rows(gt, ge[0:1, :], ge[1:2, :])
        cw = cw_all[:, cols]
        a = g_up * cw[0:1, :] + gt * cw[1:2, :] + g_dn * cw[2:3, :] + cb_ref[0, :, cols]
        hid_ref[:, cols] = ((a * jax.nn.sigmoid(a)) * u).astype(hid_ref.dtype)
    h = jnp.dot(hid_ref[...], wo_ref[...], preferred_element_type=jnp.float32)
    o_ref[...] = _layer_norm(alpha * x + h, ln_g[2:3, :], ln_b[2:3, :])


def _conv_ffn(x2d, w_in, conv_w, conv_b, w_out, ln_g, ln_b, layer, seq, tm, alpha):
    t, d = x2d.shape
    d_ff = w_out.shape[0]
    tf = FFN_CHUNK
    halo = 8
    hb = tm // halo
    n_halo = t // halo
    row = lambda i: (i, 0)
    const2 = lambda i: (0, 0)
    per_layer = lambda a: pl.BlockSpec((1,) + a.shape[1:], lambda i: (layer,) + (0,) * (a.ndim - 1))
    return pl.pallas_call(
        functools.partial(_ffn_kernel, s_blocks=seq // tm, halo=halo, alpha=alpha, d_ff=d_ff, tf=tf),
        out_shape=jax.ShapeDtypeStruct((t, d), jnp.float32),
        grid=(t // tm,),
        in_specs=[pl.BlockSpec((tm, d), row),
                  pl.BlockSpec((halo, d), lambda i: (jnp.maximum(i * hb - 1, 0), 0)),
                  pl.BlockSpec((halo, d), lambda i: (jnp.minimum((i + 1) * hb, n_halo - 1), 0)),
                  _resident(w_in.shape, const2),
                  per_layer(conv_w),
                  per_layer(conv_b),
                  _resident(w_out.shape, const2),
                  per_layer(ln_g),
                  per_layer(ln_b)],
        out_specs=pl.BlockSpec((tm, d), row),
        scratch_shapes=[pltpu.VMEM((tm, d_ff), MXU_DTYPE)],
        compiler_params=_params("parallel"),
        name="conv_ffn",
    )(x2d, x2d, x2d, w_in, conv_w, conv_b, w_out, ln_g, ln_b)

def kernel(x, mem, emb_ln_g, emb_ln_b, w_in, lam_q1, lam_k1, lam_q2, lam_k2, subln_g, rpb, sc_conv_w, w_branch,
           w_mix_out, xa_q, xa_kv, xa_o, ffn_w_in, ffn_conv_w, ffn_conv_b, ffn_w_out, ln_g, ln_b):
    batch, seq, d = x.shape
    depth = w_in.shape[0]
    mem_len = mem.shape[1]
    assert seq % GRID_W == 0 and seq // GRID_W >= NA_WIN_ROWS
    alpha = (2.0 * depth) ** 0.25
    tm = min(512, seq)
    tm_ffn = min(1024, seq)
    tq = min(512, seq)
    tk = min(1024, seq // 2)
    wcast = lambda a: a.astype(MXU_DTYPE)

    rope_tab = _rope_table(seq)
    bias_tab = _na_bias_table(rpb.reshape((depth * NA_HEADS,) + rpb.shape[2:]))
    stack3 = lambda a: a.reshape(depth, 1, a.shape[-1])
    lq1, lk1, lq2, lk2, sub_g, conv_b = (stack3(a) for a in (lam_q1, lam_k1, lam_q2, lam_k2, subln_g, ffn_conv_b))
    h = x.reshape(batch * seq, d)
    mem2d = mem.reshape(batch * mem_len, d)
    for l in range(depth):
        lam_init = 0.8 - 0.6 * math.exp(-0.3 * l)
        if l == 0:
            h, *proj = _input_proj(h, wcast(w_in[l]), rope_tab, seq, tm, emb_ln=(emb_ln_g, emb_ln_b))
        else:
            proj = _input_proj(h, wcast(w_in[l]), rope_tab, seq, tm)
        qa, ka, va, qb, kb, vb, gb, u, gates = proj
        ya = _diff_attention(qa, ka, va, lq1, lk1, lq2, lk2, sub_g, l, lam_init, batch, seq, tq, tk)
        yb = _neighbourhood_attention(qb, kb, vb, bias_tab, l, batch, seq, NA_ROWS_PER_STEP)
        kv = _mem_kv(mem2d, wcast(xa_kv[l]))
        h = _merge_xattn(h, ya, yb, gb, u, gates, sc_conv_w, wcast(w_branch[l]), wcast(w_mix_out[l]),
                         kv, wcast(xa_q[l]), wcast(xa_o[l]), ln_g, ln_b, l, seq, mem_len, tm, alpha)
        h = _conv_ffn(h, wcast(ffn_w_in[l]), ffn_conv_w, conv_b, wcast(ffn_w_out[l]), ln_g, ln_b,
                      l, seq, tm_ffn, alpha)
    return h.reshape(batch, seq, d)
```

```python
import functools
import math

import numpy as np
import jax
import jax.numpy as jnp
from jax import lax
from jax.experimental import pallas as pl
from jax.experimental.pallas import tpu as pltpu

GRID_W = 64
HEAD_DIM = 64
DA_HEADS = 4
DA_VDIM = 2 * HEAD_DIM
NA_HEADS = 8
NA_WIN_ROWS = 8
NA_WIN_COLS = 16
SC_W = 512
N_BRANCH = 3
XA_HEADS = 4
ROPE_THETA = 10000.0
LN_EPS = 1e-5

LANES = 128
MXU_DTYPE = jnp.bfloat16
ACT_DTYPE = jnp.bfloat16
NEG_BIG = -1e30
VMEM_LIMIT = 56 * 1024 * 1024
FFN_CHUNK = 256
NA_ROWS_PER_STEP = 16


def _params(*sem):
    return pltpu.CompilerParams(dimension_semantics=sem, vmem_limit_bytes=VMEM_LIMIT)


def _resident(shape, index_map):
    return pl.BlockSpec(shape, index_map, pipeline_mode=pl.Buffered(1))


def _layer_norm(v, g, b):
    mu = jnp.mean(v, axis=-1, keepdims=True)
    d = v - mu
    var = jnp.mean(d * d, axis=-1, keepdims=True)
    return d * lax.rsqrt(var + LN_EPS) * g + b


def _dot(a, b):
    return jnp.dot(a.astype(MXU_DTYPE), b.astype(MXU_DTYPE), preferred_element_type=jnp.float32)


def _dot_nt(a, b):
    return lax.dot_general(a.astype(MXU_DTYPE), b.astype(MXU_DTYPE), (((1,), (1,)), ((), ())),
                           preferred_element_type=jnp.float32)


def _proj_kernel(*refs, d_gate, embed_ln):
    if embed_ln:
        x_ref, g_ref, b_ref, w_ref, rope_ref, xn_ref, *outs = refs
        x = _layer_norm(x_ref[...], g_ref[...], b_ref[...])
        xn_ref[...] = x
    else:
        x_ref, w_ref, rope_ref, *outs = refs
        x = x_ref[...]
    qa_ref, ka_ref, va_ref, qb_ref, kb_ref, vb_ref, gb_ref, u_ref, gate_ref = outs
    xb = x.astype(MXU_DTYPE)
    tm = xb.shape[0]
    w = SC_W

    def mm(c0, c1):
        return jnp.dot(xb, w_ref[:, c0:c1], preferred_element_type=jnp.float32)

    lane = lax.broadcasted_iota(jnp.int32, (tm, LANES), 1)
    first_half = (lane % HEAD_DIM) < (HEAD_DIM // 2)

    def rope(y, cos, sin):
        outs = []
        for c in range(y.shape[1] // LANES):
            yc = y[:, c * LANES:(c + 1) * LANES]
            rot = jnp.where(first_half, pltpu.roll(yc, LANES - HEAD_DIM // 2, 1),
                            pltpu.roll(yc, HEAD_DIM // 2, 1))
            outs.append(yc * cos + rot * sin)
        return jnp.concatenate(outs, axis=1)

    tab = rope_ref[...]
    qa_ref[...] = rope(mm(0, w), tab[:, 0:LANES], tab[:, LANES:2 * LANES]).astype(qa_ref.dtype)
    ka_ref[...] = rope(mm(w, 2 * w), tab[:, 2 * LANES:3 * LANES], tab[:, 3 * LANES:4 * LANES]).astype(ka_ref.dtype)
    va_ref[...] = mm(2 * w, 3 * w).astype(va_ref.dtype)
    qb_ref[...] = (mm(3 * w, 4 * w) * (HEAD_DIM ** -0.5)).astype(qb_ref.dtype)
    kb_ref[...] = mm(4 * w, 5 * w).astype(kb_ref.dtype)
    vb_ref[...] = mm(5 * w, 6 * w).astype(vb_ref.dtype)
    gb_ref[...] = mm(6 * w, 7 * w).astype(gb_ref.dtype)
    u_ref[...] = (mm(7 * w, 8 * w) * mm(8 * w, 9 * w)).astype(u_ref.dtype)
    gate_ref[...] = jax.nn.sigmoid(mm(9 * w, 9 * w + d_gate)).astype(gate_ref.dtype)


def _input_proj(x2d, w_in, rope_tab, seq, tm, emb_ln=None):
    t, d = x2d.shape
    n_cols = w_in.shape[1]
    d_gate = n_cols - 9 * SC_W
    s_blocks = seq // tm
    row = lambda i: (i, 0)
    const2 = lambda i: (0, 0)
    small = jax.ShapeDtypeStruct((t, SC_W), ACT_DTYPE)
    ln_args, ln_specs, ln_shape, ln_out = [], [], [], []
    if emb_ln is not None:
        ln_args = [a.reshape(1, d) for a in emb_ln]
        ln_specs = [pl.BlockSpec((1, d), const2)] * 2
        ln_shape = [jax.ShapeDtypeStruct((t, d), jnp.float32)]
        ln_out = [pl.BlockSpec((tm, d), row)]
    return pl.pallas_call(
        functools.partial(_proj_kernel, d_gate=d_gate, embed_ln=emb_ln is not None),
        out_shape=ln_shape + [small] * 8 + [jax.ShapeDtypeStruct((t, d_gate), ACT_DTYPE)],
        grid=(t // tm,),
        in_specs=[pl.BlockSpec((tm, d), row)] + ln_specs + [
            _resident((d, n_cols), const2),
            pl.BlockSpec((tm, 4 * LANES), lambda i: (i % s_blocks, 0))],
        out_specs=ln_out + [pl.BlockSpec((tm, SC_W), row)] * 8 + [pl.BlockSpec((tm, d_gate), row)],
        compiler_params=_params("parallel"),
        name="input_proj",
    )(x2d, *ln_args, w_in, rope_tab)


def _rope_table(seq):
    half = HEAD_DIM // 2
    inv = ROPE_THETA ** (-jnp.arange(half, dtype=jnp.float32) * 2.0 / HEAD_DIM)
    ang = jnp.arange(seq, dtype=jnp.float32)[:, None] * inv[None, :]
    cos, sin = jnp.cos(ang), jnp.sin(ang)
    cos_l = jnp.tile(jnp.concatenate([cos, cos], axis=1), (1, LANES // HEAD_DIM))
    sin_l = jnp.tile(jnp.concatenate([-sin, sin], axis=1), (1, LANES // HEAD_DIM))
    scale = HEAD_DIM ** -0.5
    return jnp.concatenate([cos_l * scale, sin_l * scale, cos_l, sin_l], axis=1)


def _diff_attn_kernel(q_ref, k_ref, v_ref, lq1_ref, lk1_ref, lq2_ref, lk2_ref, g_ref, o_ref,
                      qs_ref, m_ref, acc_ref, *buf_refs, tq, tk, rc, n_kv, n_q, ahead, lam_init):
    lane = lax.broadcasted_iota(jnp.int32, (tq, LANES), 1)
    ones_col = jnp.where(lax.broadcasted_iota(jnp.int32, (tk, LANES), 1) == 0, 1.0, 0.0).astype(v_ref.dtype)
    lam = (jnp.exp(jnp.sum(lq1_ref[0] * lk1_ref[0], axis=1, keepdims=True))
           - jnp.exp(jnp.sum(lq2_ref[0] * lk2_ref[0], axis=1, keepdims=True)) + lam_init)
    n_buf = len(buf_refs) // 2
    s_bufs = tuple(zip(buf_refs[:n_buf], buf_refs[n_buf:]))

    def stack_queries(t):
        q = q_ref[pl.ds(pl.multiple_of(t * tq, tq), tq), :]
        zero = jnp.zeros_like(q)
        qs_ref[0:tq, :] = jnp.where(lane < HEAD_DIM, q, zero)
        qs_ref[tq:2 * tq, :] = jnp.where(lane >= HEAD_DIM, q, zero)

    def scores(j):
        s_ref, mx_ref = s_bufs[j % n_buf]
        s = _dot_nt(qs_ref[...], k_ref[j * tk:(j + 1) * tk, :])
        s_ref[...] = s
        mx_ref[...] = functools.reduce(jnp.maximum, [s[:, c * LANES:(c + 1) * LANES] for c in range(tk // LANES)])

    def update(j):
        s_ref, mx_ref = s_bufs[j % n_buf]
        v_ext = jnp.concatenate([v_ref[j * tk:(j + 1) * tk, :], ones_col], axis=1)
        for c in range(2 * tq // rc):
            rows = slice(c * rc, (c + 1) * rc)
            s = s_ref[rows, :]
            m_prev = m_ref[rows, :]
            m_new = jnp.maximum(m_prev, jnp.max(mx_ref[rows, :], axis=1, keepdims=True))
            p = jnp.exp(s - jnp.concatenate([m_new] * (tk // LANES), axis=1))
            alpha = jnp.exp(m_prev - m_new)
            pv = _dot(p, v_ext)
            acc_ref[rows, :] = acc_ref[rows, :] * jnp.concatenate([alpha, alpha], axis=1) + pv
            m_ref[rows, :] = m_new

    def finalize(t):
        acc = acc_ref[...]
        o = acc[:, 0:LANES] / acc[:, LANES:LANES + 1]
        od = o[0:tq] - lam * o[tq:2 * tq]
        ms = jnp.mean(od * od, axis=-1, keepdims=True)
        y = od * lax.rsqrt(ms + LN_EPS) * g_ref[0] * (1.0 - lam_init)
        o_ref[pl.ds(pl.multiple_of(t * tq, tq), tq), :] = y.astype(o_ref.dtype)

    stack_queries(0)
    for j in range(ahead):
        scores(j)

    def tile(t, carry):
        m_ref[...] = jnp.full(m_ref.shape, NEG_BIG, jnp.float32)
        acc_ref[...] = jnp.zeros(acc_ref.shape, jnp.float32)
        for j in range(n_kv):
            if j + ahead == n_kv:
                stack_queries(jnp.minimum(t + 1, n_q - 1))
            scores((j + ahead) % n_kv)
            update(j)
        finalize(t)
        return carry

    lax.fori_loop(0, n_q, tile, 0)


def _diff_attention(qa, ka, va, lq1, lk1, lq2, lk2, subln_g, layer, lam_init, batch, seq, tq, tk):
    t = qa.shape[0]
    n_q = seq // tq
    n_kv = seq // tk
    rc = 256
    ahead = 1
    n_buf = 2 * ahead
    assert n_kv % n_buf == 0 and (2 * tq) % rc == 0
    vspec = lambda n: pl.BlockSpec((1, 1, n), lambda b, h: (layer, 0, 0))
    head = pl.BlockSpec((seq, LANES), lambda b, h: (b, h))
    return pl.pallas_call(
        functools.partial(_diff_attn_kernel, tq=tq, tk=tk, rc=rc, n_kv=n_kv, n_q=n_q, ahead=ahead,
                          lam_init=lam_init),
        out_shape=jax.ShapeDtypeStruct((t, DA_HEADS * DA_VDIM), ACT_DTYPE),
        grid=(batch, DA_HEADS),
        in_specs=[head, head, head,
                  vspec(HEAD_DIM), vspec(HEAD_DIM), vspec(HEAD_DIM), vspec(HEAD_DIM), vspec(DA_VDIM)],
        out_specs=head,
        scratch_shapes=([pltpu.VMEM((2 * tq, LANES), qa.dtype),
                         pltpu.VMEM((2 * tq, LANES), jnp.float32),
                         pltpu.VMEM((2 * tq, 2 * LANES), jnp.float32)]
                        + [pltpu.VMEM((2 * tq, tk), jnp.float32)] * n_buf
                        + [pltpu.VMEM((2 * tq, LANES), jnp.float32)] * n_buf),
        compiler_params=_params("parallel", "parallel"),
        name="diff_attention",
    )(qa, ka, va, lq1, lk1, lq2, lk2, subln_g)


def _na_bias_table(rpb):
    nh, n_dr, n_dc = rpb.shape
    qc = np.arange(GRID_W)
    cs = np.clip(qc - NA_WIN_COLS // 2, 0, GRID_W - NA_WIN_COLS)
    kc = np.arange(GRID_W)
    valid = (kc[None, :] >= cs[:, None]) & (kc[None, :] < cs[:, None] + NA_WIN_COLS)
    p = GRID_W + NA_WIN_COLS
    row = jnp.pad(rpb.astype(jnp.float32), ((0, 0), (0, 0), (0, p - n_dc)))
    toe = jnp.tile(row, (1, 1, GRID_W))[:, :, :GRID_W * (p - 1)].reshape(nh, n_dr, GRID_W, p - 1)
    toe = toe[:, :, :, NA_WIN_COLS - 1:NA_WIN_COLS - 1 + GRID_W]
    toe = jnp.where(jnp.asarray(valid)[None, None], toe, NEG_BIG)
    hp = LANES // HEAD_DIM
    tab = toe.reshape(nh // hp, hp, n_dr, GRID_W, GRID_W).transpose(0, 2, 4, 1, 3)
    return tab.reshape(nh // hp, n_dr * GRID_W, hp * GRID_W)


def _na_kernel(q_ref, k_ref, v_ref, bias_ref, o_ref, *, rows, rows_per_step):
    i = pl.program_id(2)
    win = NA_WIN_ROWS * GRID_W
    lane = lax.broadcasted_iota(jnp.int32, (GRID_W, LANES), 1)
    lo = lane < HEAD_DIM
    for rr in range(rows_per_step):
        r = i * rows_per_step + rr
        rs = jnp.clip(r - NA_WIN_ROWS // 2, 0, rows - NA_WIN_ROWS)
        off = pl.multiple_of(rs * GRID_W, GRID_W)
        bias_off = pl.multiple_of((rs - r + NA_WIN_ROWS - 1) * GRID_W, GRID_W)
        kwin = k_ref[pl.ds(off, win), :]
        vwin = v_ref[pl.ds(off, win), :]
        q = q_ref[rr * GRID_W:(rr + 1) * GRID_W, :]
        zero = jnp.zeros_like(q)
        qst = jnp.concatenate([jnp.where(lo, q, zero), jnp.where(lo, zero, q)], axis=0)
        st = _dot_nt(kwin, qst) + bias_ref[0, pl.ds(bias_off, win), :]
        m = jnp.max(st, axis=0, keepdims=True)
        p = jnp.exp(st - m)
        l = jnp.sum(p, axis=0, keepdims=True)
        pn = (p * (1.0 / l)).astype(MXU_DTYPE)
        pv = lax.dot_general(pn, vwin.astype(MXU_DTYPE), (((0,), (0,)), ((), ())),
                             preferred_element_type=jnp.float32)
        o = jnp.where(lo, pv[0:GRID_W], pv[GRID_W:2 * GRID_W])
        o_ref[rr * GRID_W:(rr + 1) * GRID_W, :] = o.astype(o_ref.dtype)


def _neighbourhood_attention(qb, kb, vb, bias_tab, layer, batch, seq, rows_per_step):
    t = qb.shape[0]
    rows = seq // GRID_W
    tq = rows_per_step * GRID_W
    n_q = seq // tq
    pairs = NA_HEADS * HEAD_DIM // LANES
    return pl.pallas_call(
        functools.partial(_na_kernel, rows=rows, rows_per_step=rows_per_step),
        out_shape=jax.ShapeDtypeStruct((t, NA_HEADS * HEAD_DIM), ACT_DTYPE),
        grid=(batch, pairs, n_q),
        in_specs=[pl.BlockSpec((tq, LANES), lambda b, h, i: (b * n_q + i, h)),
                  pl.BlockSpec((seq, LANES), lambda b, h, i: (b, h)),
                  pl.BlockSpec((seq, LANES), lambda b, h, i: (b, h)),
                  pl.BlockSpec((1,) + bias_tab.shape[1:], lambda b, h, i: (layer * pairs + h, 0, 0))],
        out_specs=pl.BlockSpec((tq, LANES), lambda b, h, i: (b * n_q + i, h)),
        compiler_params=_params("parallel", "parallel", "parallel"),
        name="neighbourhood_attention",
    )(qb, kb, vb, bias_tab)


def _shift_rows(a, prev_row, next_row):
    n = a.shape[0]
    ridx = lax.broadcasted_iota(jnp.int32, a.shape, 0)
    up = jnp.where(ridx == 0, prev_row, pltpu.roll(a, 1, 0))
    dn = jnp.where(ridx == n - 1, next_row, pltpu.roll(a, n - 1, 0))
    return up, dn


def _merge_xattn_kernel(x_ref, ya_ref, yb_ref, gb_ref, u_ref, up_ref, un_ref, gate_ref, cw_ref, wb_ref, wm_ref,
                        k_ref, v_ref, wq_ref, wo_ref, g_ref, b_ref, o_ref,
                        *, s_blocks, halo, alpha, d_model, n_sub):
    i = pl.program_id(0)
    has_prev = i % s_blocks != 0
    has_next = i % s_blocks != s_blocks - 1
    u = u_ref[...].astype(jnp.float32)
    prev_row = jnp.where(has_prev, up_ref[halo - 1:halo, :].astype(jnp.float32), 0.0)
    next_row = jnp.where(has_next, un_ref[0:1, :].astype(jnp.float32), 0.0)
    u_up, u_dn = _shift_rows(u, prev_row, next_row)
    cw = cw_ref[0]
    conv = u_up * cw[0:1, :] + u * cw[1:2, :] + u_dn * cw[2:3, :]
    yc = (gb_ref[...].astype(jnp.float32) * conv).astype(MXU_DTYPE)
    ln_g, ln_b = g_ref[0], b_ref[0]
    k = k_ref[...]
    v = v_ref[...]
    hd = d_model // XA_HEADS
    ts = x_ref.shape[0] // n_sub
    subs = [slice(sub * ts, (sub + 1) * ts) for sub in range(n_sub)]

    def merge(rows):
        gate = gate_ref[rows, :].astype(jnp.float32)
        merged = (gate[:, 0:d_model] * _dot(ya_ref[rows, :], wb_ref[0])
                  + gate[:, d_model:2 * d_model] * _dot(yb_ref[rows, :], wb_ref[1])
                  + gate[:, 2 * d_model:3 * d_model] * _dot(yc[rows, :], wb_ref[2]))
        return alpha * x_ref[rows, :] + _dot(merged, wm_ref[...])

    def attend(q):
        outs = []
        for h in range(XA_HEADS):
            sl = slice(h * hd, (h + 1) * hd)
            s = _dot_nt(q[:, sl], k[:, sl])
            m = jnp.max(s, axis=1, keepdims=True)
            p = jnp.exp(s - m)
            l = jnp.sum(p, axis=1, keepdims=True)
            outs.append(_dot(p, v[:, sl]) / l)
        return jnp.concatenate(outs, axis=1)

    pre = [merge(rows) for rows in subs]
    x1 = [_layer_norm(a, ln_g[0:1, :], ln_b[0:1, :]) for a in pre]
    q = [_dot(a, wq_ref[...]) * (hd ** -0.5) for a in x1]
    att = [attend(a) for a in q]
    h_out = [_dot(a, wo_ref[...]) for a in att]
    for rows, a, b in zip(subs, x1, h_out):
        o_ref[rows, :] = _layer_norm(alpha * a + b, ln_g[1:2, :], ln_b[1:2, :])


def _merge_xattn(x2d, ya, yb, gb, u, gates, conv_w, w_branch, w_mix, kv, xa_q, xa_o, ln_g, ln_b,
                 layer, seq, mem_len, tm, alpha):
    t, d = x2d.shape
    halo = 16
    hb = tm // halo
    n_halo = t // halo
    s_blocks = seq // tm
    row = lambda i: (i, 0)
    const2 = lambda i: (0, 0)
    per_layer = lambda a: pl.BlockSpec((1,) + a.shape[1:], lambda i: (layer,) + (0,) * (a.ndim - 1))
    return pl.pallas_call(
        functools.partial(_merge_xattn_kernel, s_blocks=s_blocks, halo=halo, alpha=alpha, d_model=d,
                          n_sub=2 if tm % 32 == 0 else 1),
        out_shape=jax.ShapeDtypeStruct((t, d), jnp.float32),
        grid=(t // tm,),
        in_specs=[pl.BlockSpec((tm, d), row),
                  pl.BlockSpec((tm, SC_W), row),
                  pl.BlockSpec((tm, SC_W), row),
                  pl.BlockSpec((tm, SC_W), row),
                  pl.BlockSpec((tm, SC_W), row),
                  pl.BlockSpec((halo, SC_W), lambda i: (jnp.maximum(i * hb - 1, 0), 0)),
                  pl.BlockSpec((halo, SC_W), lambda i: (jnp.minimum((i + 1) * hb, n_halo - 1), 0)),
                  pl.BlockSpec((tm, N_BRANCH * d), row),
                  per_layer(conv_w),
                  _resident(w_branch.shape, lambda i: (0, 0, 0)),
                  _resident(w_mix.shape, const2),
                  pl.BlockSpec((mem_len, d), lambda i: (i // s_blocks, 0)),
                  pl.BlockSpec((mem_len, d), lambda i: (i // s_blocks, 1)),
                  _resident((d, d), const2),
                  _resident((d, d), const2),
                  per_layer(ln_g),
                  per_layer(ln_b)],
        out_specs=pl.BlockSpec((tm, d), row),
        compiler_params=_params("parallel"),
        name="merge_xattn",
    )(x2d, ya, yb, gb, u, u, u, gates, conv_w, w_branch, w_mix, kv, kv, xa_q, xa_o, ln_g, ln_b)


def _kv_kernel(m_ref, w_ref, o_ref):
    o_ref[...] = _dot(m_ref[...], w_ref[...]).astype(o_ref.dtype)


def _mem_kv(mem2d, xa_kv):
    t, d = mem2d.shape
    n = xa_kv.shape[1]
    tm = min(t, 256)
    return pl.pallas_call(
        _kv_kernel,
        out_shape=jax.ShapeDtypeStruct((t, n), ACT_DTYPE),
        grid=(t // tm,),
        in_specs=[pl.BlockSpec((tm, d), lambda i: (i, 0)), _resident((d, n), lambda i: (0, 0))],
        out_specs=pl.BlockSpec((tm, n), lambda i: (i, 0)),
        compiler_params=_params("parallel"),
        name="mem_kv",
    )(mem2d, xa_kv)


def _ffn_kernel(x_ref, xp_ref, xn_ref, wi_ref, cw_ref, cb_ref, wo_ref, g_ref, b_ref, o_ref, hid_ref,
                *, s_blocks, halo, alpha, d_ff, tf):
    i = pl.program_id(0)
    has_prev = i % s_blocks != 0
    has_next = i % s_blocks != s_blocks - 1
    x = x_ref[...]
    xb = x.astype(MXU_DTYPE)
    cw_all = cw_ref[0]
    ln_g, ln_b = g_ref[0], b_ref[0]
    xe = jnp.concatenate([jnp.where(has_prev, xp_ref[halo - 1:halo, :], 0.0),
                          jnp.where(has_next, xn_ref[0:1, :], 0.0),
                          jnp.zeros((halo - 2, x.shape[1]), jnp.float32)], axis=0).astype(MXU_DTYPE)
    for c in range(d_ff // tf):
        cols = slice(c * tf, (c + 1) * tf)
        w_c = jnp.concatenate([wi_ref[:, cols], wi_ref[:, d_ff + c * tf:d_ff + (c + 1) * tf]], axis=1)
        ug = jnp.dot(xb, w_c, preferred_element_type=jnp.float32)
        u, gt = ug[:, 0:tf], ug[:, tf:2 * tf]
        ge = jnp.dot(xe, w_c, preferred_element_type=jnp.float32)[:, tf:2 * tf]
        g_up, g_dn = _shift_rows(gt, ge[0:1, :], ge[1:2, :])
        cw = cw_all[:, cols]
        a = g_up * cw[0:1, :] + gt * cw[1:2, :] + g_dn * cw[2:3, :] + cb_ref[0, :, cols]
        hid_ref[:, cols] = ((a * jax.nn.sigmoid(a)) * u).astype(hid_ref.dtype)
    h = jnp.dot(hid_ref[...], wo_ref[...], preferred_element_type=jnp.float32)
    o_ref[...] = _layer_norm(alpha * x + h, ln_g[2:3, :], ln_b[2:3, :])


def _conv_ffn(x2d, w_in, conv_w, conv_b, w_out, ln_g, ln_b, layer, seq, tm, alpha):
    t, d = x2d.shape
    d_ff = w_out.shape[0]
    tf = FFN_CHUNK
    halo = 8
    hb = tm // halo
    n_halo = t // halo
    row = lambda i: (i, 0)
    const2 = lambda i: (0, 0)
    per_layer = lambda a: pl.BlockSpec((1,) + a.shape[1:], lambda i: (layer,) + (0,) * (a.ndim - 1))
    return pl.pallas_call(
        functools.partial(_ffn_kernel, s_blocks=seq // tm, halo=halo, alpha=alpha, d_ff=d_ff, tf=tf),
        out_shape=jax.ShapeDtypeStruct((t, d), jnp.float32),
        grid=(t // tm,),
        in_specs=[pl.BlockSpec((tm, d), row),
                  pl.BlockSpec((halo, d), lambda i: (jnp.maximum(i * hb - 1, 0), 0)),
                  pl.BlockSpec((halo, d), lambda i: (jnp.minimum((i + 1) * hb, n_halo - 1), 0)),
                  _resident(w_in.shape, const2),
                  per_layer(conv_w),
                  per_layer(conv_b),
                  _resident(w_out.shape, const2),
                  per_layer(ln_g),
                  per_layer(ln_b)],
        out_specs=pl.BlockSpec((tm, d), row),
        scratch_shapes=[pltpu.VMEM((tm, d_ff), MXU_DTYPE)],
        compiler_params=_params("parallel"),
        name="conv_ffn",
    )(x2d, x2d, x2d, w_in, conv_w, conv_b, w_out, ln_g, ln_b)

def kernel(x, mem, emb_ln_g, emb_ln_b, w_in, lam_q1, lam_k1, lam_q2, lam_k2, subln_g, rpb, sc_conv_w, w_branch,
           w_mix_out, xa_q, xa_kv, xa_o, ffn_w_in, ffn_conv_w, ffn_conv_b, ffn_w_out, ln_g, ln_b):
    batch, seq, d = x.shape
    depth = w_in.shape[0]
    mem_len = mem.shape[1]
    assert seq % GRID_W == 0 and seq // GRID_W >= NA_WIN_ROWS
    alpha = (2.0 * depth) ** 0.25
    tm = min(512, seq)
    tm_ffn = min(1024, seq)
    tq = min(512, seq)
    tk = min(2048, seq // 2)
    wcast = lambda a: a.astype(MXU_DTYPE)

    rope_tab = _rope_table(seq)
    bias_tab = _na_bias_table(rpb.reshape((depth * NA_HEADS,) + rpb.shape[2:]))
    stack3 = lambda a: a.reshape(depth, 1, a.shape[-1])
    lq1, lk1, lq2, lk2, sub_g, conv_b = (stack3(a) for a in (lam_q1, lam_k1, lam_q2, lam_k2, subln_g, ffn_conv_b))
    h = x.reshape(batch * seq, d)
    mem2d = mem.reshape(batch * mem_len, d)
    for l in range(depth):
        lam_init = 0.8 - 0.6 * math.exp(-0.3 * l)
        if l == 0:
            h, *proj = _input_proj(h, wcast(w_in[l]), rope_tab, seq, tm, emb_ln=(emb_ln_g, emb_ln_b))
        else:
            proj = _input_proj(h, wcast(w_in[l]), rope_tab, seq, tm)
        qa, ka, va, qb, kb, vb, gb, u, gates = proj
        ya = _diff_attention(qa, ka, va, lq1, lk1, lq2, lk2, sub_g, l, lam_init, batch, seq, tq, tk)
        yb = _neighbourhood_attention(qb, kb, vb, bias_tab, l, batch, seq, NA_ROWS_PER_STEP)
        kv = _mem_kv(mem2d, wcast(xa_kv[l]))
        h = _merge_xattn(h, ya, yb, gb, u, gates, sc_conv_w, wcast(w_branch[l]), wcast(w_mix_out[l]),
                         kv, wcast(xa_q[l]), wcast(xa_o[l]), ln_g, ln_b, l, seq, mem_len, tm, alpha)
        h = _conv_ffn(h, wcast(ffn_w_in[l]), ffn_conv_w, conv_b, wcast(ffn_w_out[l]), ln_g, ln_b,
                      l, seq, tm_ffn, alpha)
    return h.reshape(batch, seq, d)
```

```python
import functools
import math

import numpy as np
import jax
import jax.numpy as jnp
from jax import lax
from jax.experimental import pallas as pl
from jax.experimental.pallas import tpu as pltpu

GRID_W = 64
HEAD_DIM = 64
DA_HEADS = 4
DA_VDIM = 2 * HEAD_DIM
NA_HEADS = 8
NA_WIN_ROWS = 8
NA_WIN_COLS = 16
SC_W = 512
N_BRANCH = 3
XA_HEADS = 4
ROPE_THETA = 10000.0
LN_EPS = 1e-5

LANES = 128
MXU_DTYPE = jnp.bfloat16
ACT_DTYPE = jnp.bfloat16
NEG_BIG = -1e30
VMEM_LIMIT = 56 * 1024 * 1024
FFN_CHUNK = 256
NA_ROWS_PER_STEP = 16


def _params(*sem):
    return pltpu.CompilerParams(dimension_semantics=sem, vmem_limit_bytes=VMEM_LIMIT)


def _resident(w, layer):
    zeros = (0,) * (w.ndim - 1)
    return pl.BlockSpec((1,) + w.shape[1:], lambda *_: (layer,) + zeros, pipeline_mode=pl.Buffered(1))


def _cast_kernel(x_ref, o_ref):
    o_ref[...] = x_ref[...].astype(o_ref.dtype)


def _to_mxu_dtype(w, rows_per_step=256):
    cols = w.shape[-1]
    w2 = w.reshape(-1, cols)
    rows = w2.shape[0]
    assert rows % rows_per_step == 0
    out = pl.pallas_call(
        _cast_kernel,
        out_shape=jax.ShapeDtypeStruct((rows, cols), MXU_DTYPE),
        grid=(rows // rows_per_step,),
        in_specs=[pl.BlockSpec((rows_per_step, cols), lambda i: (i, 0))],
        out_specs=pl.BlockSpec((rows_per_step, cols), lambda i: (i, 0)),
        compiler_params=_params("parallel"),
        name="cast_weights",
    )(w2)
    return out.reshape(w.shape)


def _layer_norm(v, g, b):
    mu = jnp.mean(v, axis=-1, keepdims=True)
    d = v - mu
    var = jnp.mean(d * d, axis=-1, keepdims=True)
    return d * lax.rsqrt(var + LN_EPS) * g + b


def _dot(a, b):
    return jnp.dot(a.astype(MXU_DTYPE), b.astype(MXU_DTYPE), preferred_element_type=jnp.float32)


def _dot_nt(a, b):
    return lax.dot_general(a.astype(MXU_DTYPE), b.astype(MXU_DTYPE), (((1,), (1,)), ((), ())),
                           preferred_element_type=jnp.float32)


def _proj_kernel(*refs, d_gate, embed_ln):
    if embed_ln:
        x_ref, g_ref, b_ref, w_ref, rope_ref, xn_ref, *outs = refs
        x = _layer_norm(x_ref[...], g_ref[...], b_ref[...])
        xn_ref[...] = x
    else:
        x_ref, w_ref, rope_ref, *outs = refs
        x = x_ref[...]
    qa_ref, ka_ref, va_ref, qb_ref, kb_ref, vb_ref, gb_ref, u_ref, gate_ref = outs
    xb = x.astype(MXU_DTYPE)
    tm = xb.shape[0]
    w = SC_W

    def mm(c0, c1):
        return jnp.dot(xb, w_ref[0, :, c0:c1], preferred_element_type=jnp.float32)

    lane = lax.broadcasted_iota(jnp.int32, (tm, LANES), 1)
    first_half = (lane % HEAD_DIM) < (HEAD_DIM // 2)

    def rope(y, cos, sin):
        outs = []
        for c in range(y.shape[1] // LANES):
            yc = y[:, c * LANES:(c + 1) * LANES]
            rot = jnp.where(first_half, pltpu.roll(yc, LANES - HEAD_DIM // 2, 1),
                            pltpu.roll(yc, HEAD_DIM // 2, 1))
            outs.append(yc * cos + rot * sin)
        return jnp.concatenate(outs, axis=1)

    tab = rope_ref[...]
    qa_ref[...] = rope(mm(0, w), tab[:, 0:LANES], tab[:, LANES:2 * LANES]).astype(qa_ref.dtype)
    ka_ref[...] = rope(mm(w, 2 * w), tab[:, 2 * LANES:3 * LANES], tab[:, 3 * LANES:4 * LANES]).astype(ka_ref.dtype)
    va_ref[...] = mm(2 * w, 3 * w).astype(va_ref.dtype)
    qb_ref[...] = (mm(3 * w, 4 * w) * (HEAD_DIM ** -0.5)).astype(qb_ref.dtype)
    kb_ref[...] = mm(4 * w, 5 * w).astype(kb_ref.dtype)
    vb_ref[...] = mm(5 * w, 6 * w).astype(vb_ref.dtype)
    gb_ref[...] = mm(6 * w, 7 * w).astype(gb_ref.dtype)
    u_ref[...] = (mm(7 * w, 8 * w) * mm(8 * w, 9 * w)).astype(u_ref.dtype)
    gate_ref[...] = jax.nn.sigmoid(mm(9 * w, 9 * w + d_gate)).astype(gate_ref.dtype)


def _input_proj(x2d, w_in, layer, rope_tab, seq, tm, emb_ln=None):
    t, d = x2d.shape
    n_cols = w_in.shape[-1]
    d_gate = n_cols - 9 * SC_W
    s_blocks = seq // tm
    row = lambda i: (i, 0)
    const2 = lambda i: (0, 0)
    small = jax.ShapeDtypeStruct((t, SC_W), ACT_DTYPE)
    ln_args, ln_specs, ln_shape, ln_out = [], [], [], []
    if emb_ln is not None:
        ln_args = [a.reshape(1, d) for a in emb_ln]
        ln_specs = [pl.BlockSpec((1, d), const2)] * 2
        ln_shape = [jax.ShapeDtypeStruct((t, d), jnp.float32)]
        ln_out = [pl.BlockSpec((tm, d), row)]
    return pl.pallas_call(
        functools.partial(_proj_kernel, d_gate=d_gate, embed_ln=emb_ln is not None),
        out_shape=ln_shape + [small] * 8 + [jax.ShapeDtypeStruct((t, d_gate), ACT_DTYPE)],
        grid=(t // tm,),
        in_specs=[pl.BlockSpec((tm, d), row)] + ln_specs + [
            _resident(w_in, layer),
            pl.BlockSpec((tm, 4 * LANES), lambda i: (i % s_blocks, 0))],
        out_specs=ln_out + [pl.BlockSpec((tm, SC_W), row)] * 8 + [pl.BlockSpec((tm, d_gate), row)],
        compiler_params=_params("parallel"),
        name="input_proj",
    )(x2d, *ln_args, w_in, rope_tab)


def _rope_table(seq):
    half = HEAD_DIM // 2
    inv = ROPE_THETA ** (-jnp.arange(half, dtype=jnp.float32) * 2.0 / HEAD_DIM)
    ang = jnp.arange(seq, dtype=jnp.float32)[:, None] * inv[None, :]
    cos, sin = jnp.cos(ang), jnp.sin(ang)
    cos_l = jnp.tile(jnp.concatenate([cos, cos], axis=1), (1, LANES // HEAD_DIM))
    sin_l = jnp.tile(jnp.concatenate([-sin, sin], axis=1), (1, LANES // HEAD_DIM))
    scale = HEAD_DIM ** -0.5
    return jnp.concatenate([cos_l * scale, sin_l * scale, cos_l, sin_l], axis=1)


def _diff_attn_kernel(q_ref, k_ref, v_ref, lq1_ref, lk1_ref, lq2_ref, lk2_ref, g_ref, o_ref,
                      qs_ref, m_ref, acc_ref, *buf_refs, tq, tk, rc, n_kv, n_q, ahead, lam_init):
    lane = lax.broadcasted_iota(jnp.int32, (tq, LANES), 1)
    ones_col = jnp.where(lax.broadcasted_iota(jnp.int32, (tk, LANES), 1) == 0, 1.0, 0.0).astype(v_ref.dtype)
    lam = (jnp.exp(jnp.sum(lq1_ref[0] * lk1_ref[0], axis=1, keepdims=True))
           - jnp.exp(jnp.sum(lq2_ref[0] * lk2_ref[0], axis=1, keepdims=True)) + lam_init)
    n_buf = len(buf_refs) // 2
    s_bufs = tuple(zip(buf_refs[:n_buf], buf_refs[n_buf:]))

    def stack_queries(t):
        q = q_ref[pl.ds(pl.multiple_of(t * tq, tq), tq), :]
        zero = jnp.zeros_like(q)
        qs_ref[0:tq, :] = jnp.where(lane < HEAD_DIM, q, zero)
        qs_ref[tq:2 * tq, :] = jnp.where(lane >= HEAD_DIM, q, zero)

    def scores(j):
        s_ref, mx_ref = s_bufs[j % n_buf]
        s = _dot_nt(qs_ref[...], k_ref[j * tk:(j + 1) * tk, :])
        s_ref[...] = s
        mx_ref[...] = functools.reduce(jnp.maximum, [s[:, c * LANES:(c + 1) * LANES] for c in range(tk // LANES)])

    def update(j):
        s_ref, mx_ref = s_bufs[j % n_buf]
        v_ext = jnp.concatenate([v_ref[j * tk:(j + 1) * tk, :], ones_col], axis=1)
        for c in range(2 * tq // rc):
            rows = slice(c * rc, (c + 1) * rc)
            s = s_ref[rows, :]
            m_prev = m_ref[rows, :]
            m_new = jnp.maximum(m_prev, jnp.max(mx_ref[rows, :], axis=1, keepdims=True))
            p = jnp.exp(s - jnp.concatenate([m_new] * (tk // LANES), axis=1))
            alpha = jnp.exp(m_prev - m_new)
            pv = _dot(p, v_ext)
            acc_ref[rows, :] = acc_ref[rows, :] * jnp.concatenate([alpha, alpha], axis=1) + pv
            m_ref[rows, :] = m_new

    def finalize(t):
        acc = acc_ref[...]
        o = acc[:, 0:LANES] / acc[:, LANES:LANES + 1]
        od = o[0:tq] - lam * o[tq:2 * tq]
        ms = jnp.mean(od * od, axis=-1, keepdims=True)
        y = od * lax.rsqrt(ms + LN_EPS) * g_ref[0] * (1.0 - lam_init)
        o_ref[pl.ds(pl.multiple_of(t * tq, tq), tq), :] = y.astype(o_ref.dtype)

    stack_queries(0)
    for j in range(ahead):
        scores(j)

    def tile(t, carry):
        m_ref[...] = jnp.full(m_ref.shape, NEG_BIG, jnp.float32)
        acc_ref[...] = jnp.zeros(acc_ref.shape, jnp.float32)
        for j in range(n_kv):
            if j + ahead == n_kv:
                stack_queries(jnp.minimum(t + 1, n_q - 1))
            scores((j + ahead) % n_kv)
            update(j)
        finalize(t)
        return carry

    lax.fori_loop(0, n_q, tile, 0)


def _diff_attention(qa, ka, va, lq1, lk1, lq2, lk2, subln_g, layer, lam_init, batch, seq, tq, tk):
    t = qa.shape[0]
    n_q = seq // tq
    n_kv = seq // tk
    rc = 256
    ahead = 1
    n_buf = 2 * ahead
    assert n_kv % n_buf == 0 and (2 * tq) % rc == 0
    vspec = lambda n: pl.BlockSpec((1, 1, n), lambda b, h: (layer, 0, 0))
    head = pl.BlockSpec((seq, LANES), lambda b, h: (b, h))
    return pl.pallas_call(
        functools.partial(_diff_attn_kernel, tq=tq, tk=tk, rc=rc, n_kv=n_kv, n_q=n_q, ahead=ahead,
                          lam_init=lam_init),
        out_shape=jax.ShapeDtypeStruct((t, DA_HEADS * DA_VDIM), ACT_DTYPE),
        grid=(batch, DA_HEADS),
        in_specs=[head, head, head,
                  vspec(HEAD_DIM), vspec(HEAD_DIM), vspec(HEAD_DIM), vspec(HEAD_DIM), vspec(DA_VDIM)],
        out_specs=head,
        scratch_shapes=([pltpu.VMEM((2 * tq, LANES), qa.dtype),
                         pltpu.VMEM((2 * tq, LANES), jnp.float32),
                         pltpu.VMEM((2 * tq, 2 * LANES), jnp.float32)]
                        + [pltpu.VMEM((2 * tq, tk), jnp.float32)] * n_buf
                        + [pltpu.VMEM((2 * tq, LANES), jnp.float32)] * n_buf),
        compiler_params=_params("parallel", "parallel"),
        name="diff_attention",
    )(qa, ka, va, lq1, lk1, lq2, lk2, subln_g)


def _na_bias_table(rpb):
    nh, n_dr, n_dc = rpb.shape
    qc = np.arange(GRID_W)
    cs = np.clip(qc - NA_WIN_COLS // 2, 0, GRID_W - NA_WIN_COLS)
    kc = np.arange(GRID_W)
    valid = (kc[None, :] >= cs[:, None]) & (kc[None, :] < cs[:, None] + NA_WIN_COLS)
    p = GRID_W + NA_WIN_COLS
    row = jnp.pad(rpb.astype(jnp.float32), ((0, 0), (0, 0), (0, p - n_dc)))
    toe = jnp.tile(row, (1, 1, GRID_W))[:, :, :GRID_W * (p - 1)].reshape(nh, n_dr, GRID_W, p - 1)
    toe = toe[:, :, :, NA_WIN_COLS - 1:NA_WIN_COLS - 1 + GRID_W]
    toe = jnp.where(jnp.asarray(valid)[None, None], toe, NEG_BIG)
    hp = LANES // HEAD_DIM
    tab = toe.reshape(nh // hp, hp, n_dr, GRID_W, GRID_W).transpose(0, 2, 4, 1, 3)
    return tab.reshape(nh // hp, n_dr * GRID_W, hp * GRID_W)


def _na_kernel(q_ref, k_ref, v_ref, bias_ref, o_ref, *, rows, rows_per_step):
    i = pl.program_id(2)
    win = NA_WIN_ROWS * GRID_W
    lane = lax.broadcasted_iota(jnp.int32, (GRID_W, LANES), 1)
    lo = lane < HEAD_DIM
    def window(rr):
        r = i * rows_per_step + rr
        rs = jnp.clip(r - NA_WIN_ROWS // 2, 0, rows - NA_WIN_ROWS)
        return r, rs, pl.multiple_of(rs * GRID_W, GRID_W)

    def scores(rr):
        r, rs, off = window(rr)
        bias_off = pl.multiple_of((rs - r + NA_WIN_ROWS - 1) * GRID_W, GRID_W)
        kwin = k_ref[pl.ds(off, win), :]
        q = q_ref[rr * GRID_W:(rr + 1) * GRID_W, :]
        zero = jnp.zeros_like(q)
        qst = jnp.concatenate([jnp.where(lo, q, zero), jnp.where(lo, zero, q)], axis=0)
        return _dot_nt(kwin, qst) + bias_ref[0, pl.ds(bias_off, win), :]

    def attend(rr, st):
        _, _, off = window(rr)
        vwin = v_ref[pl.ds(off, win), :]
        m = jnp.max(st, axis=0, keepdims=True)
        p = jnp.exp(st - m)
        l = jnp.sum(p, axis=0, keepdims=True)
        pn = (p * (1.0 / l)).astype(MXU_DTYPE)
        pv = lax.dot_general(pn, vwin.astype(MXU_DTYPE), (((0,), (0,)), ((), ())),
                             preferred_element_type=jnp.float32)
        o = jnp.where(lo, pv[0:GRID_W], pv[GRID_W:2 * GRID_W])
        o_ref[rr * GRID_W:(rr + 1) * GRID_W, :] = o.astype(o_ref.dtype)

    st = scores(0)
    for rr in range(rows_per_step):
        st_next = scores(rr + 1) if rr + 1 < rows_per_step else None
        attend(rr, st)
        st = st_next


def _neighbourhood_attention(qb, kb, vb, bias_tab, layer, batch, seq, rows_per_step):
    t = qb.shape[0]
    rows = seq // GRID_W
    tq = rows_per_step * GRID_W
    n_q = seq // tq
    pairs = NA_HEADS * HEAD_DIM // LANES
    return pl.pallas_call(
        functools.partial(_na_kernel, rows=rows, rows_per_step=rows_per_step),
        out_shape=jax.ShapeDtypeStruct((t, NA_HEADS * HEAD_DIM), ACT_DTYPE),
        grid=(batch, pairs, n_q),
        in_specs=[pl.BlockSpec((tq, LANES), lambda b, h, i: (b * n_q + i, h)),
                  pl.BlockSpec((seq, LANES), lambda b, h, i: (b, h)),
                  pl.BlockSpec((seq, LANES), lambda b, h, i: (b, h)),
                  pl.BlockSpec((1,) + bias_tab.shape[1:], lambda b, h, i: (layer * pairs + h, 0, 0))],
        out_specs=pl.BlockSpec((tq, LANES), lambda b, h, i: (b * n_q + i, h)),
        compiler_params=_params("parallel", "parallel", "parallel"),
        name="neighbourhood_attention",
    )(qb, kb, vb, bias_tab)


def _shift_rows(a, prev_row, next_row):
    n = a.shape[0]
    ridx = lax.broadcasted_iota(jnp.int32, a.shape, 0)
    up = jnp.where(ridx == 0, prev_row, pltpu.roll(a, 1, 0))
    dn = jnp.where(ridx == n - 1, next_row, pltpu.roll(a, n - 1, 0))
    return up, dn


def _merge_xattn_kernel(x_ref, ya_ref, yb_ref, gb_ref, u_ref, up_ref, un_ref, gate_ref, cw_ref, wb_ref, wm_ref,
                        k_ref, v_ref, wq_ref, wo_ref, g_ref, b_ref, o_ref,
                        *, s_blocks, halo, alpha, d_model, n_sub):
    i = pl.program_id(0)
    has_prev = i % s_blocks != 0
    has_next = i % s_blocks != s_blocks - 1
    u = u_ref[...].astype(jnp.float32)
    prev_row = jnp.where(has_prev, up_ref[halo - 1:halo, :].astype(jnp.float32), 0.0)
    next_row = jnp.where(has_next, un_ref[0:1, :].astype(jnp.float32), 0.0)
    u_up, u_dn = _shift_rows(u, prev_row, next_row)
    cw = cw_ref[0]
    conv = u_up * cw[0:1, :] + u * cw[1:2, :] + u_dn * cw[2:3, :]
    yc = (gb_ref[...].astype(jnp.float32) * conv).astype(MXU_DTYPE)
    ln_g, ln_b = g_ref[0], b_ref[0]
    k = k_ref[...]
    v = v_ref[...]
    hd = d_model // XA_HEADS
    ts = x_ref.shape[0] // n_sub
    subs = [slice(sub * ts, (sub + 1) * ts) for sub in range(n_sub)]

    def merge(rows):
        gate = gate_ref[rows, :].astype(jnp.float32)
        merged = (gate[:, 0:d_model] * _dot(ya_ref[rows, :], wb_ref[0, 0])
                  + gate[:, d_model:2 * d_model] * _dot(yb_ref[rows, :], wb_ref[0, 1])
                  + gate[:, 2 * d_model:3 * d_model] * _dot(yc[rows, :], wb_ref[0, 2]))
        return alpha * x_ref[rows, :] + _dot(merged, wm_ref[0])

    def attend(q):
        outs = []
        for h in range(XA_HEADS):
            sl = slice(h * hd, (h + 1) * hd)
            s = _dot_nt(q[:, sl], k[:, sl])
            m = jnp.max(s, axis=1, keepdims=True)
            p = jnp.exp(s - m)
            l = jnp.sum(p, axis=1, keepdims=True)
            outs.append(_dot(p, v[:, sl]) / l)
        return jnp.concatenate(outs, axis=1)

    pre = [merge(rows) for rows in subs]
    x1 = [_layer_norm(a, ln_g[0:1, :], ln_b[0:1, :]) for a in pre]
    q = [_dot(a, wq_ref[0]) * (hd ** -0.5) for a in x1]
    att = [attend(a) for a in q]
    h_out = [_dot(a, wo_ref[0]) for a in att]
    for rows, a, b in zip(subs, x1, h_out):
        o_ref[rows, :] = _layer_norm(alpha * a + b, ln_g[1:2, :], ln_b[1:2, :])


def _merge_xattn(x2d, ya, yb, gb, u, gates, conv_w, w_branch, w_mix, kv, xa_q, xa_o, ln_g, ln_b,
                 layer, seq, mem_len, tm, alpha):
    t, d = x2d.shape
    halo = 16
    hb = tm // halo
    n_halo = t // halo
    s_blocks = seq // tm
    row = lambda i: (i, 0)
    const2 = lambda i: (0, 0)
    per_layer = lambda a: pl.BlockSpec((1,) + a.shape[1:], lambda i: (layer,) + (0,) * (a.ndim - 1))
    return pl.pallas_call(
        functools.partial(_merge_xattn_kernel, s_blocks=s_blocks, halo=halo, alpha=alpha, d_model=d,
                          n_sub=2 if tm % 32 == 0 else 1),
        out_shape=jax.ShapeDtypeStruct((t, d), jnp.float32),
        grid=(t // tm,),
        in_specs=[pl.BlockSpec((tm, d), row),
                  pl.BlockSpec((tm, SC_W), row),
                  pl.BlockSpec((tm, SC_W), row),
                  pl.BlockSpec((tm, SC_W), row),
                  pl.BlockSpec((tm, SC_W), row),
                  pl.BlockSpec((halo, SC_W), lambda i: (jnp.maximum(i * hb - 1, 0), 0)),
                  pl.BlockSpec((halo, SC_W), lambda i: (jnp.minimum((i + 1) * hb, n_halo - 1), 0)),
                  pl.BlockSpec((tm, N_BRANCH * d), row),
                  per_layer(conv_w),
                  _resident(w_branch, layer),
                  _resident(w_mix, layer),
                  pl.BlockSpec((mem_len, d), lambda i: (i // s_blocks, 0)),
                  pl.BlockSpec((mem_len, d), lambda i: (i // s_blocks, 1)),
                  _resident(xa_q, layer),
                  _resident(xa_o, layer),
                  per_layer(ln_g),
                  per_layer(ln_b)],
        out_specs=pl.BlockSpec((tm, d), row),
        compiler_params=_params("parallel"),
        name="merge_xattn",
    )(x2d, ya, yb, gb, u, u, u, gates, conv_w, w_branch, w_mix, kv, kv, xa_q, xa_o, ln_g, ln_b)


def _kv_kernel(m_ref, w_ref, o_ref):
    o_ref[...] = _dot(m_ref[...], w_ref[0]).astype(o_ref.dtype)


def _mem_kv(mem2d, xa_kv, layer):
    t, d = mem2d.shape
    n = xa_kv.shape[-1]
    tm = min(t, 256)
    return pl.pallas_call(
        _kv_kernel,
        out_shape=jax.ShapeDtypeStruct((t, n), ACT_DTYPE),
        grid=(t // tm,),
        in_specs=[pl.BlockSpec((tm, d), lambda i: (i, 0)), _resident(xa_kv, layer)],
        out_specs=pl.BlockSpec((tm, n), lambda i: (i, 0)),
        compiler_params=_params("parallel"),
        name="mem_kv",
    )(mem2d, xa_kv)


def _ffn_kernel(x_ref, xp_ref, xn_ref, wi_ref, cw_ref, cb_ref, wo_ref, g_ref, b_ref, o_ref, hid_ref,
                *, s_blocks, halo, alpha, d_ff, tf):
    i = pl.program_id(0)
    has_prev = i % s_blocks != 0
    has_next = i % s_blocks != s_blocks - 1
    x = x_ref[...]
    xb = x.astype(MXU_DTYPE)
    cw_all = cw_ref[0]
    ln_g, ln_b = g_ref[0], b_ref[0]
    xe = jnp.concatenate([jnp.where(has_prev, xp_ref[halo - 1:halo, :], 0.0),
                          jnp.where(has_next, xn_ref[0:1, :], 0.0),
                          jnp.zeros((halo - 2, x.shape[1]), jnp.float32)], axis=0).astype(MXU_DTYPE)
    for c in range(d_ff // tf):
        cols = slice(c * tf, (c + 1) * tf)
        w_c = jnp.concatenate([wi_ref[0, :, cols], wi_ref[0, :, d_ff + c * tf:d_ff + (c + 1) * tf]], axis=1)
        ug = jnp.dot(xb, w_c, preferred_element_type=jnp.float32)
        u, gt = ug[:, 0:tf], ug[:, tf:2 * tf]
        ge = jnp.dot(xe, w_c, preferred_element_type=jnp.float32)[:, tf:2 * tf]
        g_up, g_dn = _shift_rows(gt, ge[0:1, :], ge[1:2, :])
        cw = cw_all[:, cols]
        a = g_up * cw[0:1, :] + gt * cw[1:2, :] + g_dn * cw[2:3, :] + cb_ref[0, :, cols]
        hid_ref[:, cols] = ((a * jax.nn.sigmoid(a)) * u).astype(hid_ref.dtype)
    h = jnp.dot(hid_ref[...], wo_ref[0], preferred_element_type=jnp.float32)
    o_ref[...] = _layer_norm(alpha * x + h, ln_g[2:3, :], ln_b[2:3, :])


def _conv_ffn(x2d, w_in, conv_w, conv_b, w_out, ln_g, ln_b, layer, seq, tm, alpha):
    t, d = x2d.shape
    d_ff = w_out.shape[1]
    tf = FFN_CHUNK
    halo = 8
    hb = tm // halo
    n_halo = t // halo
    row = lambda i: (i, 0)
    const2 = lambda i: (0, 0)
    per_layer = lambda a: pl.BlockSpec((1,) + a.shape[1:], lambda i: (layer,) + (0,) * (a.ndim - 1))
    return pl.pallas_call(
        functools.partial(_ffn_kernel, s_blocks=seq // tm, halo=halo, alpha=alpha, d_ff=d_ff, tf=tf),
        out_shape=jax.ShapeDtypeStruct((t, d), jnp.float32),
        grid=(t // tm,),
        in_specs=[pl.BlockSpec((tm, d), row),
                  pl.BlockSpec((halo, d), lambda i: (jnp.maximum(i * hb - 1, 0), 0)),
                  pl.BlockSpec((halo, d), lambda i: (jnp.minimum((i + 1) * hb, n_halo - 1), 0)),
                  _resident(w_in, layer),
                  per_layer(conv_w),
                  per_layer(conv_b),
                  _resident(w_out, layer),
                  per_layer(ln_g),
                  per_layer(ln_b)],
        out_specs=pl.BlockSpec((tm, d), row),
        scratch_shapes=[pltpu.VMEM((tm, d_ff), MXU_DTYPE)],
        compiler_params=_params("parallel"),
        name="conv_ffn",
    )(x2d, x2d, x2d, w_in, conv_w, conv_b, w_out, ln_g, ln_b)

def kernel(x, mem, emb_ln_g, emb_ln_b, w_in, lam_q1, lam_k1, lam_q2, lam_k2, subln_g, rpb, sc_conv_w, w_branch,
           w_mix_out, xa_q, xa_kv, xa_o, ffn_w_in, ffn_conv_w, ffn_conv_b, ffn_w_out, ln_g, ln_b):
    batch, seq, d = x.shape
    depth = w_in.shape[0]
    mem_len = mem.shape[1]
    assert seq % GRID_W == 0 and seq // GRID_W >= NA_WIN_ROWS
    alpha = (2.0 * depth) ** 0.25
    tm = min(512, seq)
    tm_ffn = min(1024, seq)
    tq = min(512, seq)
    tk = min(2048, seq // 2)
    w_in, w_branch, w_mix_out, xa_q, xa_kv, xa_o, ffn_w_in, ffn_w_out = (
        _to_mxu_dtype(w) for w in (w_in, w_branch, w_mix_out, xa_q, xa_kv, xa_o, ffn_w_in, ffn_w_out))

    rope_tab = _rope_table(seq)
    bias_tab = _na_bias_table(rpb.reshape((depth * NA_HEADS,) + rpb.shape[2:]))
    stack3 = lambda a: a.reshape(depth, 1, a.shape[-1])
    lq1, lk1, lq2, lk2, sub_g, conv_b = (stack3(a) for a in (lam_q1, lam_k1, lam_q2, lam_k2, subln_g, ffn_conv_b))
    h = x.reshape(batch * seq, d)
    mem2d = mem.reshape(batch * mem_len, d)
    for l in range(depth):
        lam_init = 0.8 - 0.6 * math.exp(-0.3 * l)
        if l == 0:
            h, *proj = _input_proj(h, w_in, l, rope_tab, seq, tm, emb_ln=(emb_ln_g, emb_ln_b))
        else:
            proj = _input_proj(h, w_in, l, rope_tab, seq, tm)
        qa, ka, va, qb, kb, vb, gb, u, gates = proj
        ya = _diff_attention(qa, ka, va, lq1, lk1, lq2, lk2, sub_g, l, lam_init, batch, seq, tq, tk)
        yb = _neighbourhood_attention(qb, kb, vb, bias_tab, l, batch, seq, NA_ROWS_PER_STEP)
        kv = _mem_kv(mem2d, xa_kv, l)
        h = _merge_xattn(h, ya, yb, gb, u, gates, sc_conv_w, w_branch, w_mix_out, kv, xa_q, xa_o,
                         ln_g, ln_b, l, seq, mem_len, tm, alpha)
        h = _conv_ffn(h, ffn_w_in, ffn_conv_w, conv_b, ffn_w_out, ln_g, ln_b, l, seq, tm_ffn, alpha)
    return h.reshape(batch, seq, d)
```

```python
import functools
import math

import numpy as np
import jax
import jax.numpy as jnp
from jax import lax
from jax.experimental import pallas as pl
from jax.experimental.pallas import tpu as pltpu

GRID_W = 64
HEAD_DIM = 64
DA_HEADS = 4
DA_VDIM = 2 * HEAD_DIM
NA_HEADS = 8
NA_WIN_ROWS = 8
NA_WIN_COLS = 16
SC_W = 512
N_BRANCH = 3
XA_HEADS = 4
ROPE_THETA = 10000.0
LN_EPS = 1e-5

LANES = 128
MXU_DTYPE = jnp.bfloat16
ACT_DTYPE = jnp.bfloat16
NEG_BIG = -1e30
VMEM_LIMIT = 56 * 1024 * 1024
FFN_CHUNK = 256
NA_ROWS_PER_STEP = 16


def _params(*sem):
    return pltpu.CompilerParams(dimension_semantics=sem, vmem_limit_bytes=VMEM_LIMIT)


def _resident(w, layer):
    zeros = (0,) * (w.ndim - 1)
    return pl.BlockSpec((1,) + w.shape[1:], lambda *_: (layer,) + zeros, pipeline_mode=pl.Buffered(1))


def _cast_kernel(x_ref, o_ref):
    o_ref[...] = x_ref[...].astype(o_ref.dtype)


def _to_mxu_dtype(w, rows_per_step=256):
    cols = w.shape[-1]
    w2 = w.reshape(-1, cols)
    rows = w2.shape[0]
    assert rows % rows_per_step == 0
    out = pl.pallas_call(
        _cast_kernel,
        out_shape=jax.ShapeDtypeStruct((rows, cols), MXU_DTYPE),
        grid=(rows // rows_per_step,),
        in_specs=[pl.BlockSpec((rows_per_step, cols), lambda i: (i, 0))],
        out_specs=pl.BlockSpec((rows_per_step, cols), lambda i: (i, 0)),
        compiler_params=_params("parallel"),
        name="cast_weights",
    )(w2)
    return out.reshape(w.shape)


def _layer_norm(v, g, b):
    mu = jnp.mean(v, axis=-1, keepdims=True)
    d = v - mu
    var = jnp.mean(d * d, axis=-1, keepdims=True)
    return d * lax.rsqrt(var + LN_EPS) * g + b


def _dot(a, b):
    return jnp.dot(a.astype(MXU_DTYPE), b.astype(MXU_DTYPE), preferred_element_type=jnp.float32)


def _dot_nt(a, b):
    return lax.dot_general(a.astype(MXU_DTYPE), b.astype(MXU_DTYPE), (((1,), (1,)), ((), ())),
                           preferred_element_type=jnp.float32)


def _proj_kernel(*refs, d_gate, embed_ln):
    if embed_ln:
        x_ref, g_ref, b_ref, w_ref, rope_ref, xn_ref, *outs = refs
        x = _layer_norm(x_ref[...], g_ref[...], b_ref[...])
        xn_ref[...] = x
    else:
        x_ref, w_ref, rope_ref, *outs = refs
        x = x_ref[...]
    qa_ref, ka_ref, va_ref, qb_ref, kb_ref, vb_ref, gb_ref, u_ref, gate_ref = outs
    xb = x.astype(MXU_DTYPE)
    tm = xb.shape[0]
    w = SC_W

    def mm(c0, c1):
        return jnp.dot(xb, w_ref[0, :, c0:c1], preferred_element_type=jnp.float32)

    lane = lax.broadcasted_iota(jnp.int32, (tm, LANES), 1)
    first_half = (lane % HEAD_DIM) < (HEAD_DIM // 2)

    def rope(y, cos, sin):
        outs = []
        for c in range(y.shape[1] // LANES):
            yc = y[:, c * LANES:(c + 1) * LANES]
            rot = jnp.where(first_half, pltpu.roll(yc, LANES - HEAD_DIM // 2, 1),
                            pltpu.roll(yc, HEAD_DIM // 2, 1))
            outs.append(yc * cos + rot * sin)
        return jnp.concatenate(outs, axis=1)

    tab = rope_ref[...]
    qa_ref[...] = rope(mm(0, w), tab[:, 0:LANES], tab[:, LANES:2 * LANES]).astype(qa_ref.dtype)
    ka_ref[...] = rope(mm(w, 2 * w), tab[:, 2 * LANES:3 * LANES], tab[:, 3 * LANES:4 * LANES]).astype(ka_ref.dtype)
    va_ref[...] = mm(2 * w, 3 * w).astype(va_ref.dtype)
    qb_ref[...] = (mm(3 * w, 4 * w) * (HEAD_DIM ** -0.5)).astype(qb_ref.dtype)
    kb_ref[...] = mm(4 * w, 5 * w).astype(kb_ref.dtype)
    vb_ref[...] = mm(5 * w, 6 * w).astype(vb_ref.dtype)
    gb_ref[...] = mm(6 * w, 7 * w).astype(gb_ref.dtype)
    u_ref[...] = (mm(7 * w, 8 * w) * mm(8 * w, 9 * w)).astype(u_ref.dtype)
    gate_ref[...] = jax.nn.sigmoid(mm(9 * w, 9 * w + d_gate)).astype(gate_ref.dtype)


def _input_proj(x2d, w_in, layer, rope_tab, seq, tm, emb_ln=None):
    t, d = x2d.shape
    n_cols = w_in.shape[-1]
    d_gate = n_cols - 9 * SC_W
    s_blocks = seq // tm
    row = lambda i: (i, 0)
    const2 = lambda i: (0, 0)
    small = jax.ShapeDtypeStruct((t, SC_W), ACT_DTYPE)
    ln_args, ln_specs, ln_shape, ln_out = [], [], [], []
    if emb_ln is not None:
        ln_args = [a.reshape(1, d) for a in emb_ln]
        ln_specs = [pl.BlockSpec((1, d), const2)] * 2
        ln_shape = [jax.ShapeDtypeStruct((t, d), jnp.float32)]
        ln_out = [pl.BlockSpec((tm, d), row)]
    return pl.pallas_call(
        functools.partial(_proj_kernel, d_gate=d_gate, embed_ln=emb_ln is not None),
        out_shape=ln_shape + [small] * 8 + [jax.ShapeDtypeStruct((t, d_gate), ACT_DTYPE)],
        grid=(t // tm,),
        in_specs=[pl.BlockSpec((tm, d), row)] + ln_specs + [
            _resident(w_in, layer),
            pl.BlockSpec((tm, 4 * LANES), lambda i: (i % s_blocks, 0))],
        out_specs=ln_out + [pl.BlockSpec((tm, SC_W), row)] * 8 + [pl.BlockSpec((tm, d_gate), row)],
        compiler_params=_params("parallel"),
        name="input_proj",
    )(x2d, *ln_args, w_in, rope_tab)


def _rope_table(seq):
    half = HEAD_DIM // 2
    inv = ROPE_THETA ** (-jnp.arange(half, dtype=jnp.float32) * 2.0 / HEAD_DIM)
    ang = jnp.arange(seq, dtype=jnp.float32)[:, None] * inv[None, :]
    cos, sin = jnp.cos(ang), jnp.sin(ang)
    cos_l = jnp.tile(jnp.concatenate([cos, cos], axis=1), (1, LANES // HEAD_DIM))
    sin_l = jnp.tile(jnp.concatenate([-sin, sin], axis=1), (1, LANES // HEAD_DIM))
    scale = HEAD_DIM ** -0.5
    return jnp.concatenate([cos_l * scale, sin_l * scale, cos_l, sin_l], axis=1)


def _diff_attn_kernel(q_ref, k_ref, v_ref, lq1_ref, lk1_ref, lq2_ref, lk2_ref, g_ref, o_ref,
                      qs_ref, m_ref, acc_ref, *buf_refs, tq, tk, rc, n_kv, n_q, ahead, lam_init):
    lane = lax.broadcasted_iota(jnp.int32, (tq, LANES), 1)
    ones_col = jnp.where(lax.broadcasted_iota(jnp.int32, (tk, LANES), 1) == 0, 1.0, 0.0).astype(v_ref.dtype)
    lam = (jnp.exp(jnp.sum(lq1_ref[0] * lk1_ref[0], axis=1, keepdims=True))
           - jnp.exp(jnp.sum(lq2_ref[0] * lk2_ref[0], axis=1, keepdims=True)) + lam_init)
    n_buf = len(buf_refs) // 2
    s_bufs = tuple(zip(buf_refs[:n_buf], buf_refs[n_buf:]))

    def stack_queries(t):
        q = q_ref[pl.ds(pl.multiple_of(t * tq, tq), tq), :]
        zero = jnp.zeros_like(q)
        qs_ref[0:tq, :] = jnp.where(lane < HEAD_DIM, q, zero)
        qs_ref[tq:2 * tq, :] = jnp.where(lane >= HEAD_DIM, q, zero)

    def scores(j):
        s_ref, mx_ref = s_bufs[j % n_buf]
        s = _dot_nt(qs_ref[...], k_ref[j * tk:(j + 1) * tk, :])
        s_ref[...] = s
        mx_ref[...] = functools.reduce(jnp.maximum, [s[:, c * LANES:(c + 1) * LANES] for c in range(tk // LANES)])

    n_qc = tq // rc
    chunk_order = range(2 * n_qc)

    def update(j, t):
        s_ref, mx_ref = s_bufs[j % n_buf]
        v_ext = jnp.concatenate([v_ref[j * tk:(j + 1) * tk, :], ones_col], axis=1)
        for c in chunk_order:
            rows = slice(c * rc, (c + 1) * rc)
            s = s_ref[rows, :]
            m_cur = jnp.max(mx_ref[rows, :], axis=1, keepdims=True)
            if j == 0:
                m_new = jnp.broadcast_to(m_cur, (rc, LANES))
            else:
                m_prev = m_ref[rows, :]
                m_new = jnp.maximum(m_prev, m_cur)
            pv = _dot(jnp.exp(s - jnp.concatenate([m_new] * (tk // LANES), axis=1)), v_ext)
            if j == 0:
                acc_ref[rows, :] = pv
            else:
                alpha = jnp.exp(m_prev - m_new)
                acc_ref[rows, :] = acc_ref[rows, :] * jnp.concatenate([alpha, alpha], axis=1) + pv
            m_ref[rows, :] = m_new
            if j == n_kv - 1 and c >= n_qc:
                finalize(t, c - n_qc)

    def finalize(t, qchunk):
        r1 = slice(qchunk * rc, (qchunk + 1) * rc)
        r2 = slice(tq + qchunk * rc, tq + (qchunk + 1) * rc)
        a1, a2 = acc_ref[r1, :], acc_ref[r2, :]
        od = a1[:, 0:LANES] / a1[:, LANES:LANES + 1] - lam * (a2[:, 0:LANES] / a2[:, LANES:LANES + 1])
        ms = jnp.mean(od * od, axis=-1, keepdims=True)
        y = od * lax.rsqrt(ms + LN_EPS) * g_ref[0] * (1.0 - lam_init)
        o_ref[pl.ds(pl.multiple_of(t * tq + qchunk * rc, rc), rc), :] = y.astype(o_ref.dtype)

    stack_queries(0)
    for j in range(ahead):
        scores(j)

    def tile(t, carry):
        for j in range(n_kv):
            if j + ahead == n_kv:
                stack_queries(jnp.minimum(t + 1, n_q - 1))
            scores((j + ahead) % n_kv)
            update(j, t)
        return carry

    lax.fori_loop(0, n_q, tile, 0)


def _diff_attention(qa, ka, va, lq1, lk1, lq2, lk2, subln_g, layer, lam_init, batch, seq, tq, tk):
    t = qa.shape[0]
    n_q = seq // tq
    n_kv = seq // tk
    rc = 256
    ahead = 1
    n_buf = 2 * ahead
    assert n_kv % n_buf == 0 and (2 * tq) % rc == 0
    vspec = lambda n: pl.BlockSpec((1, 1, n), lambda b, h: (layer, 0, 0))
    head = pl.BlockSpec((seq, LANES), lambda b, h: (b, h))
    return pl.pallas_call(
        functools.partial(_diff_attn_kernel, tq=tq, tk=tk, rc=rc, n_kv=n_kv, n_q=n_q, ahead=ahead,
                          lam_init=lam_init),
        out_shape=jax.ShapeDtypeStruct((t, DA_HEADS * DA_VDIM), ACT_DTYPE),
        grid=(batch, DA_HEADS),
        in_specs=[head, head, head,
                  vspec(HEAD_DIM), vspec(HEAD_DIM), vspec(HEAD_DIM), vspec(HEAD_DIM), vspec(DA_VDIM)],
        out_specs=head,
        scratch_shapes=([pltpu.VMEM((2 * tq, LANES), qa.dtype),
                         pltpu.VMEM((2 * tq, LANES), jnp.float32),
                         pltpu.VMEM((2 * tq, 2 * LANES), jnp.float32)]
                        + [pltpu.VMEM((2 * tq, tk), jnp.float32)] * n_buf
                        + [pltpu.VMEM((2 * tq, LANES), jnp.float32)] * n_buf),
        compiler_params=_params("parallel", "parallel"),
        name="diff_attention",
    )(qa, ka, va, lq1, lk1, lq2, lk2, subln_g)


def _na_bias_table(rpb):
    nh, n_dr, n_dc = rpb.shape
    qc = np.arange(GRID_W)
    cs = np.clip(qc - NA_WIN_COLS // 2, 0, GRID_W - NA_WIN_COLS)
    kc = np.arange(GRID_W)
    valid = (kc[None, :] >= cs[:, None]) & (kc[None, :] < cs[:, None] + NA_WIN_COLS)
    p = GRID_W + NA_WIN_COLS
    row = jnp.pad(rpb.astype(jnp.float32), ((0, 0), (0, 0), (0, p - n_dc)))
    toe = jnp.tile(row, (1, 1, GRID_W))[:, :, :GRID_W * (p - 1)].reshape(nh, n_dr, GRID_W, p - 1)
    toe = toe[:, :, :, NA_WIN_COLS - 1:NA_WIN_COLS - 1 + GRID_W]
    toe = jnp.where(jnp.asarray(valid)[None, None], toe, NEG_BIG)
    hp = LANES // HEAD_DIM
    tab = toe.reshape(nh // hp, hp, n_dr, GRID_W, GRID_W).transpose(0, 2, 4, 1, 3)
    return tab.reshape(nh // hp, n_dr * GRID_W, hp * GRID_W)


def _na_kernel(q_ref, k_ref, v_ref, bias_ref, o_ref, *, rows, rows_per_step):
    i = pl.program_id(2)
    win = NA_WIN_ROWS * GRID_W
    lane = lax.broadcasted_iota(jnp.int32, (GRID_W, LANES), 1)
    lo = lane < HEAD_DIM
    def window(rr):
        r = i * rows_per_step + rr
        rs = jnp.clip(r - NA_WIN_ROWS // 2, 0, rows - NA_WIN_ROWS)
        return r, rs, pl.multiple_of(rs * GRID_W, GRID_W)

    def scores(rr):
        r, rs, off = window(rr)
        bias_off = pl.multiple_of((rs - r + NA_WIN_ROWS - 1) * GRID_W, GRID_W)
        kwin = k_ref[pl.ds(off, win), :]
        q = q_ref[rr * GRID_W:(rr + 1) * GRID_W, :]
        zero = jnp.zeros_like(q)
        qst = jnp.concatenate([jnp.where(lo, q, zero), jnp.where(lo, zero, q)], axis=0)
        return _dot_nt(kwin, qst) + bias_ref[0, pl.ds(bias_off, win), :]

    def attend(rr, st):
        _, _, off = window(rr)
        vwin = v_ref[pl.ds(off, win), :]
        m = jnp.max(st, axis=0, keepdims=True)
        p = jnp.exp(st - m)
        l = jnp.sum(p, axis=0, keepdims=True)
        pn = (p * (1.0 / l)).astype(MXU_DTYPE)
        pv = lax.dot_general(pn, vwin.astype(MXU_DTYPE), (((0,), (0,)), ((), ())),
                             preferred_element_type=jnp.float32)
        o = jnp.where(lo, pv[0:GRID_W], pv[GRID_W:2 * GRID_W])
        o_ref[rr * GRID_W:(rr + 1) * GRID_W, :] = o.astype(o_ref.dtype)

    st = scores(0)
    for rr in range(rows_per_step):
        st_next = scores(rr + 1) if rr + 1 < rows_per_step else None
        attend(rr, st)
        st = st_next


def _neighbourhood_attention(qb, kb, vb, bias_tab, layer, batch, seq, rows_per_step):
    t = qb.shape[0]
    rows = seq // GRID_W
    tq = rows_per_step * GRID_W
    n_q = seq // tq
    pairs = NA_HEADS * HEAD_DIM // LANES
    return pl.pallas_call(
        functools.partial(_na_kernel, rows=rows, rows_per_step=rows_per_step),
        out_shape=jax.ShapeDtypeStruct((t, NA_HEADS * HEAD_DIM), ACT_DTYPE),
        grid=(batch, pairs, n_q),
        in_specs=[pl.BlockSpec((tq, LANES), lambda b, h, i: (b * n_q + i, h)),
                  pl.BlockSpec((seq, LANES), lambda b, h, i: (b, h)),
                  pl.BlockSpec((seq, LANES), lambda b, h, i: (b, h)),
                  pl.BlockSpec((1,) + bias_tab.shape[1:], lambda b, h, i: (layer * pairs + h, 0, 0))],
        out_specs=pl.BlockSpec((tq, LANES), lambda b, h, i: (b * n_q + i, h)),
        compiler_params=_params("parallel", "parallel", "parallel"),
        name="neighbourhood_attention",
    )(qb, kb, vb, bias_tab)


def _shift_rows(a, prev_row, next_row):
    n = a.shape[0]
    ridx = lax.broadcasted_iota(jnp.int32, a.shape, 0)
    up = jnp.where(ridx == 0, prev_row, pltpu.roll(a, 1, 0))
    dn = jnp.where(ridx == n - 1, next_row, pltpu.roll(a, n - 1, 0))
    return up, dn


def _merge_xattn_kernel(x_ref, ya_ref, yb_ref, gb_ref, u_ref, up_ref, un_ref, gate_ref, cw_ref, wb_ref, wm_ref,
                        k_ref, v_ref, wq_ref, wo_ref, g_ref, b_ref, o_ref,
                        *, s_blocks, halo, alpha, d_model, n_sub):
    i = pl.program_id(0)
    has_prev = i % s_blocks != 0
    has_next = i % s_blocks != s_blocks - 1
    u = u_ref[...].astype(jnp.float32)
    prev_row = jnp.where(has_prev, up_ref[halo - 1:halo, :].astype(jnp.float32), 0.0)
    next_row = jnp.where(has_next, un_ref[0:1, :].astype(jnp.float32), 0.0)
    u_up, u_dn = _shift_rows(u, prev_row, next_row)
    cw = cw_ref[0]
    conv = u_up * cw[0:1, :] + u * cw[1:2, :] + u_dn * cw[2:3, :]
    yc = (gb_ref[...].astype(jnp.float32) * conv).astype(MXU_DTYPE)
    ln_g, ln_b = g_ref[0], b_ref[0]
    k = k_ref[...]
    v = v_ref[...]
    hd = d_model // XA_HEADS
    ts = x_ref.shape[0] // n_sub
    subs = [slice(sub * ts, (sub + 1) * ts) for sub in range(n_sub)]

    def merge(rows):
        gate = gate_ref[rows, :].astype(jnp.float32)
        merged = (gate[:, 0:d_model] * _dot(ya_ref[rows, :], wb_ref[0, 0])
                  + gate[:, d_model:2 * d_model] * _dot(yb_ref[rows, :], wb_ref[0, 1])
                  + gate[:, 2 * d_model:3 * d_model] * _dot(yc[rows, :], wb_ref[0, 2]))
        return alpha * x_ref[rows, :] + _dot(merged, wm_ref[0])

    def attend(q):
        outs = []
        for h in range(XA_HEADS):
            sl = slice(h * hd, (h + 1) * hd)
            s = _dot_nt(q[:, sl], k[:, sl])
            m = jnp.max(s, axis=1, keepdims=True)
            p = jnp.exp(s - m)
            l = jnp.sum(p, axis=1, keepdims=True)
            outs.append(_dot(p, v[:, sl]) / l)
        return jnp.concatenate(outs, axis=1)

    pre = [merge(rows) for rows in subs]
    x1 = [_layer_norm(a, ln_g[0:1, :], ln_b[0:1, :]) for a in pre]
    q = [_dot(a, wq_ref[0]) * (hd ** -0.5) for a in x1]
    att = [attend(a) for a in q]
    h_out = [_dot(a, wo_ref[0]) for a in att]
    for rows, a, b in zip(subs, x1, h_out):
        o_ref[rows, :] = _layer_norm(alpha * a + b, ln_g[1:2, :], ln_b[1:2, :])


def _merge_xattn(x2d, ya, yb, gb, u, gates, conv_w, w_branch, w_mix, kv, xa_q, xa_o, ln_g, ln_b,
                 layer, seq, mem_len, tm, alpha):
    t, d = x2d.shape
    halo = 16
    hb = tm // halo
    n_halo = t // halo
    s_blocks = seq // tm
    row = lambda i: (i, 0)
    const2 = lambda i: (0, 0)
    per_layer = lambda a: pl.BlockSpec((1,) + a.shape[1:], lambda i: (layer,) + (0,) * (a.ndim - 1))
    return pl.pallas_call(
        functools.partial(_merge_xattn_kernel, s_blocks=s_blocks, halo=halo, alpha=alpha, d_model=d,
                          n_sub=2 if tm % 32 == 0 else 1),
        out_shape=jax.ShapeDtypeStruct((t, d), jnp.float32),
        grid=(t // tm,),
        in_specs=[pl.BlockSpec((tm, d), row),
                  pl.BlockSpec((tm, SC_W), row),
                  pl.BlockSpec((tm, SC_W), row),
                  pl.BlockSpec((tm, SC_W), row),
                  pl.BlockSpec((tm, SC_W), row),
                  pl.BlockSpec((halo, SC_W), lambda i: (jnp.maximum(i * hb - 1, 0), 0)),
                  pl.BlockSpec((halo, SC_W), lambda i: (jnp.minimum((i + 1) * hb, n_halo - 1), 0)),
                  pl.BlockSpec((tm, N_BRANCH * d), row),
                  per_layer(conv_w),
                  _resident(w_branch, layer),
                  _resident(w_mix, layer),
                  pl.BlockSpec((mem_len, d), lambda i: (i // s_blocks, 0)),
                  pl.BlockSpec((mem_len, d), lambda i: (i // s_blocks, 1)),
                  _resident(xa_q, layer),
                  _resident(xa_o, layer),
                  per_layer(ln_g),
                  per_layer(ln_b)],
        out_specs=pl.BlockSpec((tm, d), row),
        compiler_params=_params("parallel"),
        name="merge_xattn",
    )(x2d, ya, yb, gb, u, u, u, gates, conv_w, w_branch, w_mix, kv, kv, xa_q, xa_o, ln_g, ln_b)


def _kv_kernel(m_ref, w_ref, o_ref):
    o_ref[...] = _dot(m_ref[...], w_ref[0]).astype(o_ref.dtype)


def _mem_kv(mem2d, xa_kv, layer):
    t, d = mem2d.shape
    n = xa_kv.shape[-1]
    tm = min(t, 256)
    return pl.pallas_call(
        _kv_kernel,
        out_shape=jax.ShapeDtypeStruct((t, n), ACT_DTYPE),
        grid=(t // tm,),
        in_specs=[pl.BlockSpec((tm, d), lambda i: (i, 0)), _resident(xa_kv, layer)],
        out_specs=pl.BlockSpec((tm, n), lambda i: (i, 0)),
        compiler_params=_params("parallel"),
        name="mem_kv",
    )(mem2d, xa_kv)


def _ffn_kernel(x_ref, xp_ref, xn_ref, wi_ref, cw_ref, cb_ref, wo_ref, g_ref, b_ref, o_ref, hid_ref,
                *, s_blocks, halo, alpha, d_ff, tf):
    i = pl.program_id(0)
    has_prev = i % s_blocks != 0
    has_next = i % s_blocks != s_blocks - 1
    x = x_ref[...]
    xb = x.astype(MXU_DTYPE)
    cw_all = cw_ref[0]
    ln_g, ln_b = g_ref[0], b_ref[0]
    xe = jnp.concatenate([jnp.where(has_prev, xp_ref[halo - 1:halo, :], 0.0),
                          jnp.where(has_next, xn_ref[0:1, :], 0.0),
                          jnp.zeros((halo - 2, x.shape[1]), jnp.float32)], axis=0).astype(MXU_DTYPE)
    for c in range(d_ff // tf):
        cols = slice(c * tf, (c + 1) * tf)
        w_c = jnp.concatenate([wi_ref[0, :, cols], wi_ref[0, :, d_ff + c * tf:d_ff + (c + 1) * tf]], axis=1)
        ug = jnp.dot(xb, w_c, preferred_element_type=jnp.float32)
        u, gt = ug[:, 0:tf], ug[:, tf:2 * tf]
        ge = jnp.dot(xe, w_c, preferred_element_type=jnp.float32)[:, tf:2 * tf]
        g_up, g_dn = _shift_rows(gt, ge[0:1, :], ge[1:2, :])
        cw = cw_all[:, cols]
        a = g_up * cw[0:1, :] + gt * cw[1:2, :] + g_dn * cw[2:3, :] + cb_ref[0, :, cols]
        hid_ref[:, cols] = ((a * jax.nn.sigmoid(a)) * u).astype(hid_ref.dtype)
    half = x.shape[0] // 2
    halves = (slice(0, half), slice(half, 2 * half))
    h = [jnp.dot(hid_ref[r, :], wo_ref[0], preferred_element_type=jnp.float32) for r in halves]
    for r, hr in zip(halves, h):
        o_ref[r, :] = _layer_norm(alpha * x[r, :] + hr, ln_g[2:3, :], ln_b[2:3, :])


def _conv_ffn(x2d, w_in, conv_w, conv_b, w_out, ln_g, ln_b, layer, seq, tm, alpha):
    t, d = x2d.shape
    d_ff = w_out.shape[1]
    tf = FFN_CHUNK
    halo = 8
    hb = tm // halo
    n_halo = t // halo
    row = lambda i: (i, 0)
    const2 = lambda i: (0, 0)
    per_layer = lambda a: pl.BlockSpec((1,) + a.shape[1:], lambda i: (layer,) + (0,) * (a.ndim - 1))
    return pl.pallas_call(
        functools.partial(_ffn_kernel, s_blocks=seq // tm, halo=halo, alpha=alpha, d_ff=d_ff, tf=tf),
        out_shape=jax.ShapeDtypeStruct((t, d), jnp.float32),
        grid=(t // tm,),
        in_specs=[pl.BlockSpec((tm, d), row),
                  pl.BlockSpec((halo, d), lambda i: (jnp.maximum(i * hb - 1, 0), 0)),
                  pl.BlockSpec((halo, d), lambda i: (jnp.minimum((i + 1) * hb, n_halo - 1), 0)),
                  _resident(w_in, layer),
                  per_layer(conv_w),
                  per_layer(conv_b),
                  _resident(w_out, layer),
                  per_layer(ln_g),
                  per_layer(ln_b)],
        out_specs=pl.BlockSpec((tm, d), row),
        scratch_shapes=[pltpu.VMEM((tm, d_ff), MXU_DTYPE)],
        compiler_params=_params("parallel"),
        name="conv_ffn",
    )(x2d, x2d, x2d, w_in, conv_w, conv_b, w_out, ln_g, ln_b)

def kernel(x, mem, emb_ln_g, emb_ln_b, w_in, lam_q1, lam_k1, lam_q2, lam_k2, subln_g, rpb, sc_conv_w, w_branch,
           w_mix_out, xa_q, xa_kv, xa_o, ffn_w_in, ffn_conv_w, ffn_conv_b, ffn_w_out, ln_g, ln_b):
    batch, seq, d = x.shape
    depth = w_in.shape[0]
    mem_len = mem.shape[1]
    assert seq % GRID_W == 0 and seq // GRID_W >= NA_WIN_ROWS
    alpha = (2.0 * depth) ** 0.25
    tm = min(512, seq)
    tm_ffn = min(1024, seq)
    tq = min(512, seq)
    tk = min(2048, seq // 2)
    w_in, w_branch, w_mix_out, xa_q, xa_kv, xa_o, ffn_w_in, ffn_w_out = (
        _to_mxu_dtype(w) for w in (w_in, w_branch, w_mix_out, xa_q, xa_kv, xa_o, ffn_w_in, ffn_w_out))

    rope_tab = _rope_table(seq)
    bias_tab = _na_bias_table(rpb.reshape((depth * NA_HEADS,) + rpb.shape[2:]))
    stack3 = lambda a: a.reshape(depth, 1, a.shape[-1])
    lq1, lk1, lq2, lk2, sub_g, conv_b = (stack3(a) for a in (lam_q1, lam_k1, lam_q2, lam_k2, subln_g, ffn_conv_b))
    h = x.reshape(batch * seq, d)
    mem2d = mem.reshape(batch * mem_len, d)
    for l in range(depth):
        lam_init = 0.8 - 0.6 * math.exp(-0.3 * l)
        if l == 0:
            h, *proj = _input_proj(h, w_in, l, rope_tab, seq, tm, emb_ln=(emb_ln_g, emb_ln_b))
        else:
            proj = _input_proj(h, w_in, l, rope_tab, seq, tm)
        qa, ka, va, qb, kb, vb, gb, u, gates = proj
        ya = _diff_attention(qa, ka, va, lq1, lk1, lq2, lk2, sub_g, l, lam_init, batch, seq, tq, tk)
        yb = _neighbourhood_attention(qb, kb, vb, bias_tab, l, batch, seq, NA_ROWS_PER_STEP)
        kv = _mem_kv(mem2d, xa_kv, l)
        h = _merge_xattn(h, ya, yb, gb, u, gates, sc_conv_w, w_branch, w_mix_out, kv, xa_q, xa_o,
                         ln_g, ln_b, l, seq, mem_len, tm, alpha)
        h = _conv_ffn(h, ffn_w_in, ffn_conv_w, conv_b, ffn_w_out, ln_g, ln_b, l, seq, tm_ffn, alpha)
    return h.reshape(batch, seq, d)
```

```python
import functools
import math

import numpy as np
import jax
import jax.numpy as jnp
from jax import lax
from jax.experimental import pallas as pl
from jax.experimental.pallas import tpu as pltpu

GRID_W = 64
HEAD_DIM = 64
DA_HEADS = 4
DA_VDIM = 2 * HEAD_DIM
NA_HEADS = 8
NA_WIN_ROWS = 8
NA_WIN_COLS = 16
SC_W = 512
N_BRANCH = 3
XA_HEADS = 4
ROPE_THETA = 10000.0
LN_EPS = 1e-5

LANES = 128
MXU_DTYPE = jnp.bfloat16
ACT_DTYPE = jnp.bfloat16
NEG_BIG = -1e30
VMEM_LIMIT = 56 * 1024 * 1024
FFN_CHUNK = 256
NA_ROWS_PER_STEP = 16


def _params(*sem):
    return pltpu.CompilerParams(dimension_semantics=sem, vmem_limit_bytes=VMEM_LIMIT)


def _resident(w, layer):
    zeros = (0,) * (w.ndim - 1)
    return pl.BlockSpec((1,) + w.shape[1:], lambda *_: (layer,) + zeros, pipeline_mode=pl.Buffered(1))


def _cast_kernel(x_ref, o_ref):
    o_ref[...] = x_ref[...].astype(o_ref.dtype)


def _to_mxu_dtype(w, rows_per_step=256):
    cols = w.shape[-1]
    w2 = w.reshape(-1, cols)
    rows = w2.shape[0]
    assert rows % rows_per_step == 0
    out = pl.pallas_call(
        _cast_kernel,
        out_shape=jax.ShapeDtypeStruct((rows, cols), MXU_DTYPE),
        grid=(rows // rows_per_step,),
        in_specs=[pl.BlockSpec((rows_per_step, cols), lambda i: (i, 0))],
        out_specs=pl.BlockSpec((rows_per_step, cols), lambda i: (i, 0)),
        compiler_params=_params("parallel"),
        name="cast_weights",
    )(w2)
    return out.reshape(w.shape)


def _layer_norm(v, g, b):
    mu = jnp.mean(v, axis=-1, keepdims=True)
    d = v - mu
    var = jnp.mean(d * d, axis=-1, keepdims=True)
    return d * lax.rsqrt(var + LN_EPS) * g + b


def _dot(a, b):
    return jnp.dot(a.astype(MXU_DTYPE), b.astype(MXU_DTYPE), preferred_element_type=jnp.float32)


def _dot_nt(a, b):
    return lax.dot_general(a.astype(MXU_DTYPE), b.astype(MXU_DTYPE), (((1,), (1,)), ((), ())),
                           preferred_element_type=jnp.float32)


def _proj_kernel(*refs, d_gate, embed_ln):
    if embed_ln:
        x_ref, g_ref, b_ref, w_ref, rope_ref, xn_ref, *outs = refs
        x = _layer_norm(x_ref[...], g_ref[...], b_ref[...])
        xn_ref[...] = x
    else:
        x_ref, w_ref, rope_ref, *outs = refs
        x = x_ref[...]
    qa_ref, ka_ref, va_ref, qb_ref, kb_ref, vb_ref, gb_ref, u_ref, gate_ref = outs
    xb = x.astype(MXU_DTYPE)
    tm = xb.shape[0]
    w = SC_W

    def mm(c0, c1):
        return jnp.dot(xb, w_ref[0, :, c0:c1], preferred_element_type=jnp.float32)

    lane = lax.broadcasted_iota(jnp.int32, (tm, LANES), 1)
    first_half = (lane % HEAD_DIM) < (HEAD_DIM // 2)

    def rope(y, cos, sin):
        outs = []
        for c in range(y.shape[1] // LANES):
            yc = y[:, c * LANES:(c + 1) * LANES]
            rot = jnp.where(first_half, pltpu.roll(yc, LANES - HEAD_DIM // 2, 1),
                            pltpu.roll(yc, HEAD_DIM // 2, 1))
            outs.append(yc * cos + rot * sin)
        return jnp.concatenate(outs, axis=1)

    tab = rope_ref[...]
    qa_ref[...] = rope(mm(0, w), tab[:, 0:LANES], tab[:, LANES:2 * LANES]).astype(qa_ref.dtype)
    ka_ref[...] = rope(mm(w, 2 * w), tab[:, 2 * LANES:3 * LANES], tab[:, 3 * LANES:4 * LANES]).astype(ka_ref.dtype)
    va_ref[...] = mm(2 * w, 3 * w).astype(va_ref.dtype)
    qb_ref[...] = (mm(3 * w, 4 * w) * (HEAD_DIM ** -0.5)).astype(qb_ref.dtype)
    kb_ref[...] = mm(4 * w, 5 * w).astype(kb_ref.dtype)
    vb_ref[...] = mm(5 * w, 6 * w).astype(vb_ref.dtype)
    gb_ref[...] = mm(6 * w, 7 * w).astype(gb_ref.dtype)
    u_ref[...] = (mm(7 * w, 8 * w) * mm(8 * w, 9 * w)).astype(u_ref.dtype)
    gate_ref[...] = jax.nn.sigmoid(mm(9 * w, 9 * w + d_gate)).astype(gate_ref.dtype)


def _input_proj(x2d, w_in, layer, rope_tab, seq, tm, emb_ln=None):
    t, d = x2d.shape
    n_cols = w_in.shape[-1]
    d_gate = n_cols - 9 * SC_W
    s_blocks = seq // tm
    row = lambda i: (i, 0)
    const2 = lambda i: (0, 0)
    small = jax.ShapeDtypeStruct((t, SC_W), ACT_DTYPE)
    ln_args, ln_specs, ln_shape, ln_out = [], [], [], []
    if emb_ln is not None:
        ln_args = [a.reshape(1, d) for a in emb_ln]
        ln_specs = [pl.BlockSpec((1, d), const2)] * 2
        ln_shape = [jax.ShapeDtypeStruct((t, d), jnp.float32)]
        ln_out = [pl.BlockSpec((tm, d), row)]
    return pl.pallas_call(
        functools.partial(_proj_kernel, d_gate=d_gate, embed_ln=emb_ln is not None),
        out_shape=ln_shape + [small] * 8 + [jax.ShapeDtypeStruct((t, d_gate), ACT_DTYPE)],
        grid=(t // tm,),
        in_specs=[pl.BlockSpec((tm, d), row)] + ln_specs + [
            _resident(w_in, layer),
            pl.BlockSpec((tm, 4 * LANES), lambda i: (i % s_blocks, 0))],
        out_specs=ln_out + [pl.BlockSpec((tm, SC_W), row)] * 8 + [pl.BlockSpec((tm, d_gate), row)],
        compiler_params=_params("parallel"),
        name="input_proj",
    )(x2d, *ln_args, w_in, rope_tab)


def _rope_table(seq):
    half = HEAD_DIM // 2
    inv = ROPE_THETA ** (-jnp.arange(half, dtype=jnp.float32) * 2.0 / HEAD_DIM)
    ang = jnp.arange(seq, dtype=jnp.float32)[:, None] * inv[None, :]
    cos, sin = jnp.cos(ang), jnp.sin(ang)
    cos_l = jnp.tile(jnp.concatenate([cos, cos], axis=1), (1, LANES // HEAD_DIM))
    sin_l = jnp.tile(jnp.concatenate([-sin, sin], axis=1), (1, LANES // HEAD_DIM))
    scale = HEAD_DIM ** -0.5
    return jnp.concatenate([cos_l * scale, sin_l * scale, cos_l, sin_l], axis=1)


def _diff_attn_kernel(q_ref, k_ref, v_ref, lq1_ref, lk1_ref, lq2_ref, lk2_ref, g_ref, o_ref,
                      qs_ref, m_ref, acc_ref, *buf_refs, tq, tk, rc, n_kv, n_q, ahead, lam_init):
    lane = lax.broadcasted_iota(jnp.int32, (tq, LANES), 1)
    ones_col = jnp.where(lax.broadcasted_iota(jnp.int32, (tk, LANES), 1) == 0, 1.0, 0.0).astype(v_ref.dtype)
    lam = (jnp.exp(jnp.sum(lq1_ref[0] * lk1_ref[0], axis=1, keepdims=True))
           - jnp.exp(jnp.sum(lq2_ref[0] * lk2_ref[0], axis=1, keepdims=True)) + lam_init)
    n_buf = len(buf_refs) // 2
    s_bufs = tuple(zip(buf_refs[:n_buf], buf_refs[n_buf:]))

    def stack_queries(t):
        q = q_ref[pl.ds(pl.multiple_of(t * tq, tq), tq), :]
        zero = jnp.zeros_like(q)
        qs_ref[0:tq, :] = jnp.where(lane < HEAD_DIM, q, zero)
        qs_ref[tq:2 * tq, :] = jnp.where(lane >= HEAD_DIM, q, zero)

    def scores(j):
        s_ref, mx_ref = s_bufs[j % n_buf]
        s = _dot_nt(qs_ref[...], k_ref[j * tk:(j + 1) * tk, :])
        s_ref[...] = s
        mx_ref[...] = functools.reduce(jnp.maximum, [s[:, c * LANES:(c + 1) * LANES] for c in range(tk // LANES)])

    n_qc = tq // rc
    chunk_order = range(2 * n_qc)

    def update(j, t):
        s_ref, mx_ref = s_bufs[j % n_buf]
        v_ext = jnp.concatenate([v_ref[j * tk:(j + 1) * tk, :], ones_col], axis=1)
        for c in chunk_order:
            rows = slice(c * rc, (c + 1) * rc)
            s = s_ref[rows, :]
            m_cur = jnp.max(mx_ref[rows, :], axis=1, keepdims=True)
            if j == 0:
                m_new = jnp.broadcast_to(m_cur, (rc, LANES))
            else:
                m_prev = m_ref[rows, :]
                m_new = jnp.maximum(m_prev, m_cur)
            pv = _dot(jnp.exp(s - jnp.concatenate([m_new] * (tk // LANES), axis=1)), v_ext)
            if j == 0:
                acc_ref[rows, :] = pv
            else:
                alpha = jnp.exp(m_prev - m_new)
                acc_ref[rows, :] = acc_ref[rows, :] * jnp.concatenate([alpha, alpha], axis=1) + pv
            m_ref[rows, :] = m_new
            if j == n_kv - 1 and c >= n_qc:
                finalize(t, c - n_qc)

    def finalize(t, qchunk):
        r1 = slice(qchunk * rc, (qchunk + 1) * rc)
        r2 = slice(tq + qchunk * rc, tq + (qchunk + 1) * rc)
        a1, a2 = acc_ref[r1, :], acc_ref[r2, :]
        od = a1[:, 0:LANES] / a1[:, LANES:LANES + 1] - lam * (a2[:, 0:LANES] / a2[:, LANES:LANES + 1])
        ms = jnp.mean(od * od, axis=-1, keepdims=True)
        y = od * lax.rsqrt(ms + LN_EPS) * g_ref[0] * (1.0 - lam_init)
        o_ref[pl.ds(pl.multiple_of(t * tq + qchunk * rc, rc), rc), :] = y.astype(o_ref.dtype)

    stack_queries(0)
    for j in range(ahead):
        scores(j)

    def tile(t, carry):
        for j in range(n_kv):
            if j + ahead == n_kv:
                stack_queries(jnp.minimum(t + 1, n_q - 1))
            scores((j + ahead) % n_kv)
            update(j, t)
        return carry

    lax.fori_loop(0, n_q, tile, 0)


def _diff_attention(qa, ka, va, lq1, lk1, lq2, lk2, subln_g, layer, lam_init, batch, seq, tq, tk):
    t = qa.shape[0]
    n_q = seq // tq
    n_kv = seq // tk
    rc = 256
    ahead = 1
    n_buf = 2 * ahead
    assert n_kv % n_buf == 0 and (2 * tq) % rc == 0
    vspec = lambda n: pl.BlockSpec((1, 1, n), lambda b, h: (layer, 0, 0))
    head = pl.BlockSpec((seq, LANES), lambda b, h: (b, h))
    return pl.pallas_call(
        functools.partial(_diff_attn_kernel, tq=tq, tk=tk, rc=rc, n_kv=n_kv, n_q=n_q, ahead=ahead,
                          lam_init=lam_init),
        out_shape=jax.ShapeDtypeStruct((t, DA_HEADS * DA_VDIM), ACT_DTYPE),
        grid=(batch, DA_HEADS),
        in_specs=[head, head, head,
                  vspec(HEAD_DIM), vspec(HEAD_DIM), vspec(HEAD_DIM), vspec(HEAD_DIM), vspec(DA_VDIM)],
        out_specs=head,
        scratch_shapes=([pltpu.VMEM((2 * tq, LANES), qa.dtype),
                         pltpu.VMEM((2 * tq, LANES), jnp.float32),
                         pltpu.VMEM((2 * tq, 2 * LANES), jnp.float32)]
                        + [pltpu.VMEM((2 * tq, tk), jnp.float32)] * n_buf
                        + [pltpu.VMEM((2 * tq, LANES), jnp.float32)] * n_buf),
        compiler_params=_params("parallel", "parallel"),
        name="diff_attention",
    )(qa, ka, va, lq1, lk1, lq2, lk2, subln_g)


def _na_bias_table(rpb):
    nh, n_dr, n_dc = rpb.shape
    qc = np.arange(GRID_W)
    cs = np.clip(qc - NA_WIN_COLS // 2, 0, GRID_W - NA_WIN_COLS)
    kc = np.arange(GRID_W)
    valid = (kc[None, :] >= cs[:, None]) & (kc[None, :] < cs[:, None] + NA_WIN_COLS)
    hp = LANES // HEAD_DIM
    p = GRID_W + NA_WIN_COLS
    lo = NA_WIN_COLS - 1
    row = jnp.pad(rpb.astype(jnp.float32)[:, :, ::-1], ((0, 0), (0, 0), (0, p - n_dc)))
    row = row.reshape(nh // hp, hp, n_dr, p).transpose(0, 2, 1, 3).reshape(nh // hp, n_dr, hp * p)
    n = hp * p
    toe = jnp.tile(row, (1, 1, GRID_W))[:, :, :GRID_W * (n - 1)].reshape(nh // hp, n_dr, GRID_W, n - 1)
    tab = jnp.concatenate([toe[..., h * p + lo:h * p + lo + GRID_W] for h in range(hp)], axis=-1)
    tab = jnp.where(jnp.asarray(np.tile(valid.T, (1, hp)))[None, None], tab, NEG_BIG)
    return tab.reshape(nh // hp, n_dr * GRID_W, hp * GRID_W)


def _na_kernel(q_ref, k_ref, v_ref, bias_ref, o_ref, *, rows, rows_per_step):
    i = pl.program_id(2)
    win = NA_WIN_ROWS * GRID_W
    lane = lax.broadcasted_iota(jnp.int32, (GRID_W, LANES), 1)
    lo = lane < HEAD_DIM
    def window(rr):
        r = i * rows_per_step + rr
        rs = jnp.clip(r - NA_WIN_ROWS // 2, 0, rows - NA_WIN_ROWS)
        return r, rs, pl.multiple_of(rs * GRID_W, GRID_W)

    def scores(rr):
        r, rs, off = window(rr)
        bias_off = pl.multiple_of((rs - r + NA_WIN_ROWS - 1) * GRID_W, GRID_W)
        kwin = k_ref[pl.ds(off, win), :]
        q = q_ref[rr * GRID_W:(rr + 1) * GRID_W, :]
        zero = jnp.zeros_like(q)
        qst = jnp.concatenate([jnp.where(lo, q, zero), jnp.where(lo, zero, q)], axis=0)
        return _dot_nt(kwin, qst) + bias_ref[0, pl.ds(bias_off, win), :]

    def attend(rr, st):
        _, _, off = window(rr)
        vwin = v_ref[pl.ds(off, win), :]
        m = jnp.max(st, axis=0, keepdims=True)
        p = jnp.exp(st - m)
        l = jnp.sum(p, axis=0, keepdims=True)
        pn = (p * (1.0 / l)).astype(MXU_DTYPE)
        pv = lax.dot_general(pn, vwin.astype(MXU_DTYPE), (((0,), (0,)), ((), ())),
                             preferred_element_type=jnp.float32)
        o = jnp.where(lo, pv[0:GRID_W], pv[GRID_W:2 * GRID_W])
        o_ref[rr * GRID_W:(rr + 1) * GRID_W, :] = o.astype(o_ref.dtype)

    st = scores(0)
    for rr in range(rows_per_step):
        st_next = scores(rr + 1) if rr + 1 < rows_per_step else None
        attend(rr, st)
        st = st_next


def _neighbourhood_attention(qb, kb, vb, bias_tab, layer, batch, seq, rows_per_step):
    t = qb.shape[0]
    rows = seq // GRID_W
    tq = rows_per_step * GRID_W
    n_q = seq // tq
    pairs = NA_HEADS * HEAD_DIM // LANES
    return pl.pallas_call(
        functools.partial(_na_kernel, rows=rows, rows_per_step=rows_per_step),
        out_shape=jax.ShapeDtypeStruct((t, NA_HEADS * HEAD_DIM), ACT_DTYPE),
        grid=(batch, pairs, n_q),
        in_specs=[pl.BlockSpec((tq, LANES), lambda b, h, i: (b * n_q + i, h)),
                  pl.BlockSpec((seq, LANES), lambda b, h, i: (b, h)),
                  pl.BlockSpec((seq, LANES), lambda b, h, i: (b, h)),
                  pl.BlockSpec((1,) + bias_tab.shape[1:], lambda b, h, i: (layer * pairs + h, 0, 0))],
        out_specs=pl.BlockSpec((tq, LANES), lambda b, h, i: (b * n_q + i, h)),
        compiler_params=_params("parallel", "parallel", "parallel"),
        name="neighbourhood_attention",
    )(qb, kb, vb, bias_tab)


def _shift_rows(a, prev_row, next_row):
    n = a.shape[0]
    ridx = lax.broadcasted_iota(jnp.int32, a.shape, 0)
    up = jnp.where(ridx == 0, prev_row, pltpu.roll(a, 1, 0))
    dn = jnp.where(ridx == n - 1, next_row, pltpu.roll(a, n - 1, 0))
    return up, dn


def _merge_xattn_kernel(x_ref, ya_ref, yb_ref, gb_ref, u_ref, up_ref, un_ref, gate_ref, cw_ref, wb_ref, wm_ref,
                        k_ref, v_ref, wq_ref, wo_ref, g_ref, b_ref, o_ref,
                        *, s_blocks, halo, alpha, d_model, n_sub):
    i = pl.program_id(0)
    has_prev = i % s_blocks != 0
    has_next = i % s_blocks != s_blocks - 1
    u = u_ref[...].astype(jnp.float32)
    prev_row = jnp.where(has_prev, up_ref[halo - 1:halo, :].astype(jnp.float32), 0.0)
    next_row = jnp.where(has_next, un_ref[0:1, :].astype(jnp.float32), 0.0)
    u_up, u_dn = _shift_rows(u, prev_row, next_row)
    cw = cw_ref[0]
    conv = u_up * cw[0:1, :] + u * cw[1:2, :] + u_dn * cw[2:3, :]
    yc = (gb_ref[...].astype(jnp.float32) * conv).astype(MXU_DTYPE)
    ln_g, ln_b = g_ref[0], b_ref[0]
    k = k_ref[...]
    v = v_ref[...]
    hd = d_model // XA_HEADS
    ts = x_ref.shape[0] // n_sub
    subs = [slice(sub * ts, (sub + 1) * ts) for sub in range(n_sub)]

    def merge(rows):
        gate = gate_ref[rows, :].astype(jnp.float32)
        merged = (gate[:, 0:d_model] * _dot(ya_ref[rows, :], wb_ref[0, 0])
                  + gate[:, d_model:2 * d_model] * _dot(yb_ref[rows, :], wb_ref[0, 1])
                  + gate[:, 2 * d_model:3 * d_model] * _dot(yc[rows, :], wb_ref[0, 2]))
        return alpha * x_ref[rows, :] + _dot(merged, wm_ref[0])

    def attend(q):
        outs = []
        for h in range(XA_HEADS):
            sl = slice(h * hd, (h + 1) * hd)
            s = _dot_nt(q[:, sl], k[:, sl])
            m = jnp.max(s, axis=1, keepdims=True)
            p = jnp.exp(s - m)
            l = jnp.sum(p, axis=1, keepdims=True)
            outs.append(_dot(p, v[:, sl]) / l)
        return jnp.concatenate(outs, axis=1)

    pre = [merge(rows) for rows in subs]
    x1 = [_layer_norm(a, ln_g[0:1, :], ln_b[0:1, :]) for a in pre]
    q = [_dot(a, wq_ref[0]) * (hd ** -0.5) for a in x1]
    att = [attend(a) for a in q]
    h_out = [_dot(a, wo_ref[0]) for a in att]
    for rows, a, b in zip(subs, x1, h_out):
        o_ref[rows, :] = _layer_norm(alpha * a + b, ln_g[1:2, :], ln_b[1:2, :])


def _merge_xattn(x2d, ya, yb, gb, u, gates, conv_w, w_branch, w_mix, kv, xa_q, xa_o, ln_g, ln_b,
                 layer, seq, mem_len, tm, alpha):
    t, d = x2d.shape
    halo = 16
    hb = tm // halo
    n_halo = t // halo
    s_blocks = seq // tm
    row = lambda i: (i, 0)
    const2 = lambda i: (0, 0)
    per_layer = lambda a: pl.BlockSpec((1,) + a.shape[1:], lambda i: (layer,) + (0,) * (a.ndim - 1))
    return pl.pallas_call(
        functools.partial(_merge_xattn_kernel, s_blocks=s_blocks, halo=halo, alpha=alpha, d_model=d,
                          n_sub=2 if tm % 32 == 0 else 1),
        out_shape=jax.ShapeDtypeStruct((t, d), jnp.float32),
        grid=(t // tm,),
        in_specs=[pl.BlockSpec((tm, d), row),
                  pl.BlockSpec((tm, SC_W), row),
                  pl.BlockSpec((tm, SC_W), row),
                  pl.BlockSpec((tm, SC_W), row),
                  pl.BlockSpec((tm, SC_W), row),
                  pl.BlockSpec((halo, SC_W), lambda i: (jnp.maximum(i * hb - 1, 0), 0)),
                  pl.BlockSpec((halo, SC_W), lambda i: (jnp.minimum((i + 1) * hb, n_halo - 1), 0)),
                  pl.BlockSpec((tm, N_BRANCH * d), row),
                  per_layer(conv_w),
                  _resident(w_branch, layer),
                  _resident(w_mix, layer),
                  pl.BlockSpec((mem_len, d), lambda i: (i // s_blocks, 0)),
                  pl.BlockSpec((mem_len, d), lambda i: (i // s_blocks, 1)),
                  _resident(xa_q, layer),
                  _resident(xa_o, layer),
                  per_layer(ln_g),
                  per_layer(ln_b)],
        out_specs=pl.BlockSpec((tm, d), row),
        compiler_params=_params("parallel"),
        name="merge_xattn",
    )(x2d, ya, yb, gb, u, u, u, gates, conv_w, w_branch, w_mix, kv, kv, xa_q, xa_o, ln_g, ln_b)


def _kv_kernel(m_ref, w_ref, o_ref):
    o_ref[...] = _dot(m_ref[...], w_ref[0]).astype(o_ref.dtype)


def _mem_kv(mem2d, xa_kv, layer):
    t, d = mem2d.shape
    n = xa_kv.shape[-1]
    tm = min(t, 256)
    return pl.pallas_call(
        _kv_kernel,
        out_shape=jax.ShapeDtypeStruct((t, n), ACT_DTYPE),
        grid=(t // tm,),
        in_specs=[pl.BlockSpec((tm, d), lambda i: (i, 0)), _resident(xa_kv, layer)],
        out_specs=pl.BlockSpec((tm, n), lambda i: (i, 0)),
        compiler_params=_params("parallel"),
        name="mem_kv",
    )(mem2d, xa_kv)


def _ffn_kernel(x_ref, xp_ref, xn_ref, wi_ref, cw_ref, cb_ref, wo_ref, g_ref, b_ref, o_ref, hid_ref,
                *, s_blocks, halo, alpha, d_ff, tf):
    i = pl.program_id(0)
    has_prev = i % s_blocks != 0
    has_next = i % s_blocks != s_blocks - 1
    x = x_ref[...]
    xb = x.astype(MXU_DTYPE)
    cw_all = cw_ref[0]
    ln_g, ln_b = g_ref[0], b_ref[0]
    xe = jnp.concatenate([jnp.where(has_prev, xp_ref[halo - 1:halo, :], 0.0),
                          jnp.where(has_next, xn_ref[0:1, :], 0.0),
                          jnp.zeros((halo - 2, x.shape[1]), jnp.float32)], axis=0).astype(MXU_DTYPE)
    for c in range(d_ff // tf):
        cols = slice(c * tf, (c + 1) * tf)
        w_c = jnp.concatenate([wi_ref[0, :, cols], wi_ref[0, :, d_ff + c * tf:d_ff + (c + 1) * tf]], axis=1)
        ug = jnp.dot(xb, w_c, preferred_element_type=jnp.float32)
        u, gt = ug[:, 0:tf], ug[:, tf:2 * tf]
        ge = jnp.dot(xe, w_c, preferred_element_type=jnp.float32)[:, tf:2 * tf]
        g_up, g_dn = _shift_rows(gt, ge[0:1, :], ge[1:2, :])
        cw = cw_all[:, cols]
        a = g_up * cw[0:1, :] + gt * cw[1:2, :] + g_dn * cw[2:3, :] + cb_ref[0, :, cols]
        hid_ref[:, cols] = ((a * jax.nn.sigmoid(a)) * u).astype(hid_ref.dtype)
    half = x.shape[0] // 2
    halves = (slice(0, half), slice(half, 2 * half))
    h = [jnp.dot(hid_ref[r, :], wo_ref[0], preferred_element_type=jnp.float32) for r in halves]
    for r, hr in zip(halves, h):
        o_ref[r, :] = _layer_norm(alpha * x[r, :] + hr, ln_g[2:3, :], ln_b[2:3, :])


def _conv_ffn(x2d, w_in, conv_w, conv_b, w_out, ln_g, ln_b, layer, seq, tm, alpha):
    t, d = x2d.shape
    d_ff = w_out.shape[1]
    tf = FFN_CHUNK
    halo = 8
    hb = tm // halo
    n_halo = t // halo
    row = lambda i: (i, 0)
    const2 = lambda i: (0, 0)
    per_layer = lambda a: pl.BlockSpec((1,) + a.shape[1:], lambda i: (layer,) + (0,) * (a.ndim - 1))
    return pl.pallas_call(
        functools.partial(_ffn_kernel, s_blocks=seq // tm, halo=halo, alpha=alpha, d_ff=d_ff, tf=tf),
        out_shape=jax.ShapeDtypeStruct((t, d), jnp.float32),
        grid=(t // tm,),
        in_specs=[pl.BlockSpec((tm, d), row),
                  pl.BlockSpec((halo, d), lambda i: (jnp.maximum(i * hb - 1, 0), 0)),
                  pl.BlockSpec((halo, d), lambda i: (jnp.minimum((i + 1) * hb, n_halo - 1), 0)),
                  _resident(w_in, layer),
                  per_layer(conv_w),
                  per_layer(conv_b),
                  _resident(w_out, layer),
                  per_layer(ln_g),
                  per_layer(ln_b)],
        out_specs=pl.BlockSpec((tm, d), row),
        scratch_shapes=[pltpu.VMEM((tm, d_ff), MXU_DTYPE)],
        compiler_params=_params("parallel"),
        name="conv_ffn",
    )(x2d, x2d, x2d, w_in, conv_w, conv_b, w_out, ln_g, ln_b)

def kernel(x, mem, emb_ln_g, emb_ln_b, w_in, lam_q1, lam_k1, lam_q2, lam_k2, subln_g, rpb, sc_conv_w, w_branch,
           w_mix_out, xa_q, xa_kv, xa_o, ffn_w_in, ffn_conv_w, ffn_conv_b, ffn_w_out, ln_g, ln_b):
    batch, seq, d = x.shape
    depth = w_in.shape[0]
    mem_len = mem.shape[1]
    assert seq % GRID_W == 0 and seq // GRID_W >= NA_WIN_ROWS
    alpha = (2.0 * depth) ** 0.25
    tm = min(512, seq)
    tm_ffn = min(1024, seq)
    tq = min(512, seq)
    tk = min(2048, seq // 2)
    w_in, w_branch, w_mix_out, xa_q, xa_kv, xa_o, ffn_w_in, ffn_w_out = (
        _to_mxu_dtype(w) for w in (w_in, w_branch, w_mix_out, xa_q, xa_kv, xa_o, ffn_w_in, ffn_w_out))

    rope_tab = _rope_table(seq)
    bias_tab = _na_bias_table(rpb.reshape((depth * NA_HEADS,) + rpb.shape[2:]))
    stack3 = lambda a: a.reshape(depth, 1, a.shape[-1])
    lq1, lk1, lq2, lk2, sub_g, conv_b = (stack3(a) for a in (lam_q1, lam_k1, lam_q2, lam_k2, subln_g, ffn_conv_b))
    h = x.reshape(batch * seq, d)
    mem2d = mem.reshape(batch * mem_len, d)
    for l in range(depth):
        lam_init = 0.8 - 0.6 * math.exp(-0.3 * l)
        if l == 0:
            h, *proj = _input_proj(h, w_in, l, rope_tab, seq, tm, emb_ln=(emb_ln_g, emb_ln_b))
        else:
            proj = _input_proj(h, w_in, l, rope_tab, seq, tm)
        qa, ka, va, qb, kb, vb, gb, u, gates = proj
        ya = _diff_attention(qa, ka, va, lq1, lk1, lq2, lk2, sub_g, l, lam_init, batch, seq, tq, tk)
        yb = _neighbourhood_attention(qb, kb, vb, bias_tab, l, batch, seq, NA_ROWS_PER_STEP)
        kv = _mem_kv(mem2d, xa_kv, l)
        h = _merge_xattn(h, ya, yb, gb, u, gates, sc_conv_w, w_branch, w_mix_out, kv, xa_q, xa_o,
                         ln_g, ln_b, l, seq, mem_len, tm, alpha)
        h = _conv_ffn(h, ffn_w_in, ffn_conv_w, conv_b, ffn_w_out, ln_g, ln_b, l, seq, tm_ffn, alpha)
    return h.reshape(batch, seq, d)
```

```python
import functools
import math

import numpy as np
import jax
import jax.numpy as jnp
from jax import lax
from jax.experimental import pallas as pl
from jax.experimental.pallas import tpu as pltpu

GRID_W = 64
HEAD_DIM = 64
DA_HEADS = 4
DA_VDIM = 2 * HEAD_DIM
NA_HEADS = 8
NA_WIN_ROWS = 8
NA_WIN_COLS = 16
SC_W = 512
N_BRANCH = 3
XA_HEADS = 4
ROPE_THETA = 10000.0
LN_EPS = 1e-5

LANES = 128
MXU_DTYPE = jnp.bfloat16
ACT_DTYPE = jnp.bfloat16
NEG_BIG = -1e30
VMEM_LIMIT = 56 * 1024 * 1024
FFN_CHUNK = 256
NA_ROWS_PER_STEP = 32


def _params(*sem):
    return pltpu.CompilerParams(dimension_semantics=sem, vmem_limit_bytes=VMEM_LIMIT)


def _resident(w, layer):
    zeros = (0,) * (w.ndim - 1)
    return pl.BlockSpec((1,) + w.shape[1:], lambda *_: (layer,) + zeros, pipeline_mode=pl.Buffered(1))


def _cast_kernel(x_ref, o_ref):
    o_ref[...] = x_ref[...].astype(o_ref.dtype)


def _to_mxu_dtype(w, rows_per_step=256):
    cols = w.shape[-1]
    w2 = w.reshape(-1, cols)
    rows = w2.shape[0]
    assert rows % rows_per_step == 0
    out = pl.pallas_call(
        _cast_kernel,
        out_shape=jax.ShapeDtypeStruct((rows, cols), MXU_DTYPE),
        grid=(rows // rows_per_step,),
        in_specs=[pl.BlockSpec((rows_per_step, cols), lambda i: (i, 0))],
        out_specs=pl.BlockSpec((rows_per_step, cols), lambda i: (i, 0)),
        compiler_params=_params("parallel"),
        name="cast_weights",
    )(w2)
    return out.reshape(w.shape)


def _layer_norm(v, g, b):
    mu = jnp.mean(v, axis=-1, keepdims=True)
    d = v - mu
    var = jnp.mean(d * d, axis=-1, keepdims=True)
    return d * lax.rsqrt(var + LN_EPS) * g + b


def _dot(a, b):
    return jnp.dot(a.astype(MXU_DTYPE), b.astype(MXU_DTYPE), preferred_element_type=jnp.float32)


def _dot_nt(a, b):
    return lax.dot_general(a.astype(MXU_DTYPE), b.astype(MXU_DTYPE), (((1,), (1,)), ((), ())),
                           preferred_element_type=jnp.float32)


def _proj_kernel(*refs, d_gate, embed_ln):
    if embed_ln:
        x_ref, g_ref, b_ref, w_ref, rope_ref, xn_ref, *outs = refs
        x = _layer_norm(x_ref[...], g_ref[...], b_ref[...])
        xn_ref[...] = x
    else:
        x_ref, w_ref, rope_ref, *outs = refs
        x = x_ref[...]
    qa_ref, ka_ref, va_ref, qb_ref, kb_ref, vb_ref, gb_ref, u_ref, gate_ref = outs
    xb = x.astype(MXU_DTYPE)
    tm = xb.shape[0]
    w = SC_W

    def mm(c0, c1):
        return jnp.dot(xb, w_ref[0, :, c0:c1], preferred_element_type=jnp.float32)

    lane = lax.broadcasted_iota(jnp.int32, (tm, LANES), 1)
    first_half = (lane % HEAD_DIM) < (HEAD_DIM // 2)

    def rope(y, cos, sin):
        outs = []
        for c in range(y.shape[1] // LANES):
            yc = y[:, c * LANES:(c + 1) * LANES]
            rot = jnp.where(first_half, pltpu.roll(yc, LANES - HEAD_DIM // 2, 1),
                            pltpu.roll(yc, HEAD_DIM // 2, 1))
            outs.append(yc * cos + rot * sin)
        return jnp.concatenate(outs, axis=1)

    tab = rope_ref[...]
    qa_ref[...] = rope(mm(0, w), tab[:, 0:LANES], tab[:, LANES:2 * LANES]).astype(qa_ref.dtype)
    ka_ref[...] = rope(mm(w, 2 * w), tab[:, 2 * LANES:3 * LANES], tab[:, 3 * LANES:4 * LANES]).astype(ka_ref.dtype)
    va_ref[...] = mm(2 * w, 3 * w).astype(va_ref.dtype)
    qb_ref[...] = (mm(3 * w, 4 * w) * (HEAD_DIM ** -0.5)).astype(qb_ref.dtype)
    kb_ref[...] = mm(4 * w, 5 * w).astype(kb_ref.dtype)
    vb_ref[...] = mm(5 * w, 6 * w).astype(vb_ref.dtype)
    gb_ref[...] = mm(6 * w, 7 * w).astype(gb_ref.dtype)
    u_ref[...] = (mm(7 * w, 8 * w) * mm(8 * w, 9 * w)).astype(u_ref.dtype)
    gate_ref[...] = jax.nn.sigmoid(mm(9 * w, 9 * w + d_gate)).astype(gate_ref.dtype)


def _input_proj(x2d, w_in, layer, rope_tab, seq, tm, emb_ln=None):
    t, d = x2d.shape
    n_cols = w_in.shape[-1]
    d_gate = n_cols - 9 * SC_W
    s_blocks = seq // tm
    row = lambda i: (i, 0)
    const2 = lambda i: (0, 0)
    small = jax.ShapeDtypeStruct((t, SC_W), ACT_DTYPE)
    ln_args, ln_specs, ln_shape, ln_out = [], [], [], []
    if emb_ln is not None:
        ln_args = [a.reshape(1, d) for a in emb_ln]
        ln_specs = [pl.BlockSpec((1, d), const2)] * 2
        ln_shape = [jax.ShapeDtypeStruct((t, d), jnp.float32)]
        ln_out = [pl.BlockSpec((tm, d), row)]
    return pl.pallas_call(
        functools.partial(_proj_kernel, d_gate=d_gate, embed_ln=emb_ln is not None),
        out_shape=ln_shape + [small] * 8 + [jax.ShapeDtypeStruct((t, d_gate), ACT_DTYPE)],
        grid=(t // tm,),
        in_specs=[pl.BlockSpec((tm, d), row)] + ln_specs + [
            _resident(w_in, layer),
            pl.BlockSpec((tm, 4 * LANES), lambda i: (i % s_blocks, 0))],
        out_specs=ln_out + [pl.BlockSpec((tm, SC_W), row)] * 8 + [pl.BlockSpec((tm, d_gate), row)],
        compiler_params=_params("parallel"),
        name="input_proj",
    )(x2d, *ln_args, w_in, rope_tab)


def _rope_table(seq):
    half = HEAD_DIM // 2
    inv = ROPE_THETA ** (-jnp.arange(half, dtype=jnp.float32) * 2.0 / HEAD_DIM)
    ang = jnp.arange(seq, dtype=jnp.float32)[:, None] * inv[None, :]
    cos, sin = jnp.cos(ang), jnp.sin(ang)
    cos_l = jnp.tile(jnp.concatenate([cos, cos], axis=1), (1, LANES // HEAD_DIM))
    sin_l = jnp.tile(jnp.concatenate([-sin, sin], axis=1), (1, LANES // HEAD_DIM))
    scale = HEAD_DIM ** -0.5
    return jnp.concatenate([cos_l * scale, sin_l * scale, cos_l, sin_l], axis=1)


def _diff_attn_kernel(q_ref, k_ref, v_ref, lq1_ref, lk1_ref, lq2_ref, lk2_ref, g_ref, o_ref,
                      qs_ref, m_ref, acc_ref, *buf_refs, tq, tk, rc, n_kv, n_q, ahead, lam_init):
    lane = lax.broadcasted_iota(jnp.int32, (tq, LANES), 1)
    ones_col = jnp.where(lax.broadcasted_iota(jnp.int32, (tk, LANES), 1) == 0, 1.0, 0.0).astype(v_ref.dtype)
    lam = (jnp.exp(jnp.sum(lq1_ref[0] * lk1_ref[0], axis=1, keepdims=True))
           - jnp.exp(jnp.sum(lq2_ref[0] * lk2_ref[0], axis=1, keepdims=True)) + lam_init)
    n_buf = len(buf_refs) // 2
    s_bufs = tuple(zip(buf_refs[:n_buf], buf_refs[n_buf:]))

    def stack_queries(t):
        q = q_ref[pl.ds(pl.multiple_of(t * tq, tq), tq), :]
        zero = jnp.zeros_like(q)
        qs_ref[0:tq, :] = jnp.where(lane < HEAD_DIM, q, zero)
        qs_ref[tq:2 * tq, :] = jnp.where(lane >= HEAD_DIM, q, zero)

    def scores(j):
        s_ref, mx_ref = s_bufs[j % n_buf]
        s = _dot_nt(qs_ref[...], k_ref[j * tk:(j + 1) * tk, :])
        s_ref[...] = s
        mx_ref[...] = functools.reduce(jnp.maximum, [s[:, c * LANES:(c + 1) * LANES] for c in range(tk // LANES)])

    n_qc = tq // rc
    chunk_order = range(2 * n_qc)

    def update(j, t):
        s_ref, mx_ref = s_bufs[j % n_buf]
        v_ext = jnp.concatenate([v_ref[j * tk:(j + 1) * tk, :], ones_col], axis=1)
        for c in chunk_order:
            rows = slice(c * rc, (c + 1) * rc)
            s = s_ref[rows, :]
            m_cur = jnp.max(mx_ref[rows, :], axis=1, keepdims=True)
            if j == 0:
                m_new = jnp.broadcast_to(m_cur, (rc, LANES))
            else:
                m_prev = m_ref[rows, :]
                m_new = jnp.maximum(m_prev, m_cur)
            pv = _dot(jnp.exp(s - jnp.concatenate([m_new] * (tk // LANES), axis=1)), v_ext)
            if j == 0:
                acc_ref[rows, :] = pv
            else:
                alpha = jnp.exp(m_prev - m_new)
                acc_ref[rows, :] = acc_ref[rows, :] * jnp.concatenate([alpha, alpha], axis=1) + pv
            m_ref[rows, :] = m_new
            if j == n_kv - 1 and c >= n_qc:
                finalize(t, c - n_qc)

    def finalize(t, qchunk):
        r1 = slice(qchunk * rc, (qchunk + 1) * rc)
        r2 = slice(tq + qchunk * rc, tq + (qchunk + 1) * rc)
        a1, a2 = acc_ref[r1, :], acc_ref[r2, :]
        od = a1[:, 0:LANES] / a1[:, LANES:LANES + 1] - lam * (a2[:, 0:LANES] / a2[:, LANES:LANES + 1])
        ms = jnp.mean(od * od, axis=-1, keepdims=True)
        y = od * lax.rsqrt(ms + LN_EPS) * g_ref[0] * (1.0 - lam_init)
        o_ref[pl.ds(pl.multiple_of(t * tq + qchunk * rc, rc), rc), :] = y.astype(o_ref.dtype)

    stack_queries(0)
    for j in range(ahead):
        scores(j)

    def tile(t, carry):
        for j in range(n_kv):
            if j + ahead == n_kv:
                stack_queries(jnp.minimum(t + 1, n_q - 1))
            scores((j + ahead) % n_kv)
            update(j, t)
        return carry

    lax.fori_loop(0, n_q, tile, 0)


def _diff_attention(qa, ka, va, lq1, lk1, lq2, lk2, subln_g, layer, lam_init, batch, seq, tq, tk):
    t = qa.shape[0]
    n_q = seq // tq
    n_kv = seq // tk
    rc = 256
    ahead = 1
    n_buf = 2 * ahead
    assert n_kv % n_buf == 0 and (2 * tq) % rc == 0
    vspec = lambda n: pl.BlockSpec((1, 1, n), lambda b, h: (layer, 0, 0))
    head = pl.BlockSpec((seq, LANES), lambda b, h: (b, h))
    return pl.pallas_call(
        functools.partial(_diff_attn_kernel, tq=tq, tk=tk, rc=rc, n_kv=n_kv, n_q=n_q, ahead=ahead,
                          lam_init=lam_init),
        out_shape=jax.ShapeDtypeStruct((t, DA_HEADS * DA_VDIM), ACT_DTYPE),
        grid=(batch, DA_HEADS),
        in_specs=[head, head, head,
                  vspec(HEAD_DIM), vspec(HEAD_DIM), vspec(HEAD_DIM), vspec(HEAD_DIM), vspec(DA_VDIM)],
        out_specs=head,
        scratch_shapes=([pltpu.VMEM((2 * tq, LANES), qa.dtype),
                         pltpu.VMEM((2 * tq, LANES), jnp.float32),
                         pltpu.VMEM((2 * tq, 2 * LANES), jnp.float32)]
                        + [pltpu.VMEM((2 * tq, tk), jnp.float32)] * n_buf
                        + [pltpu.VMEM((2 * tq, LANES), jnp.float32)] * n_buf),
        compiler_params=_params("parallel", "parallel"),
        name="diff_attention",
    )(qa, ka, va, lq1, lk1, lq2, lk2, subln_g)


def _na_bias_table(rpb):
    nh, n_dr, n_dc = rpb.shape
    qc = np.arange(GRID_W)
    cs = np.clip(qc - NA_WIN_COLS // 2, 0, GRID_W - NA_WIN_COLS)
    kc = np.arange(GRID_W)
    valid = (kc[None, :] >= cs[:, None]) & (kc[None, :] < cs[:, None] + NA_WIN_COLS)
    hp = LANES // HEAD_DIM
    p = GRID_W + NA_WIN_COLS
    lo = NA_WIN_COLS - 1
    row = jnp.pad(rpb.astype(jnp.float32)[:, :, ::-1], ((0, 0), (0, 0), (0, p - n_dc)))
    row = row.reshape(nh // hp, hp, n_dr, p).transpose(0, 2, 1, 3).reshape(nh // hp, n_dr, hp * p)
    n = hp * p
    toe = jnp.tile(row, (1, 1, GRID_W))[:, :, :GRID_W * (n - 1)].reshape(nh // hp, n_dr, GRID_W, n - 1)
    tab = jnp.concatenate([toe[..., h * p + lo:h * p + lo + GRID_W] for h in range(hp)], axis=-1)
    tab = jnp.where(jnp.asarray(np.tile(valid.T, (1, hp)))[None, None], tab, NEG_BIG)
    return tab.reshape(nh // hp, n_dr * GRID_W, hp * GRID_W)


def _na_kernel(q_ref, k_ref, v_ref, bias_ref, o_ref, *, rows, rows_per_step):
    i = pl.program_id(2)
    win = NA_WIN_ROWS * GRID_W
    lane = lax.broadcasted_iota(jnp.int32, (GRID_W, LANES), 1)
    lo = lane < HEAD_DIM
    def window(rr):
        r = i * rows_per_step + rr
        rs = jnp.clip(r - NA_WIN_ROWS // 2, 0, rows - NA_WIN_ROWS)
        return r, rs, pl.multiple_of(rs * GRID_W, GRID_W)

    def scores(rr):
        r, rs, off = window(rr)
        bias_off = pl.multiple_of((rs - r + NA_WIN_ROWS - 1) * GRID_W, GRID_W)
        kwin = k_ref[pl.ds(off, win), :]
        q = q_ref[rr * GRID_W:(rr + 1) * GRID_W, :]
        zero = jnp.zeros_like(q)
        qst = jnp.concatenate([jnp.where(lo, q, zero), jnp.where(lo, zero, q)], axis=0)
        return _dot_nt(kwin, qst) + bias_ref[0, pl.ds(bias_off, win), :]

    def attend(rr, st):
        _, _, off = window(rr)
        vwin = v_ref[pl.ds(off, win), :]
        m = jnp.max(st, axis=0, keepdims=True)
        p = jnp.exp(st - m)
        l = jnp.sum(p, axis=0, keepdims=True)
        pn = (p * (1.0 / l)).astype(MXU_DTYPE)
        pv = lax.dot_general(pn, vwin.astype(MXU_DTYPE), (((0,), (0,)), ((), ())),
                             preferred_element_type=jnp.float32)
        o = jnp.where(lo, pv[0:GRID_W], pv[GRID_W:2 * GRID_W])
        o_ref[rr * GRID_W:(rr + 1) * GRID_W, :] = o.astype(o_ref.dtype)

    st = scores(0)
    for rr in range(rows_per_step):
        st_next = scores(rr + 1) if rr + 1 < rows_per_step else None
        attend(rr, st)
        st = st_next


def _neighbourhood_attention(qb, kb, vb, bias_tab, layer, batch, seq, rows_per_step):
    t = qb.shape[0]
    rows = seq // GRID_W
    tq = rows_per_step * GRID_W
    n_q = seq // tq
    pairs = NA_HEADS * HEAD_DIM // LANES
    return pl.pallas_call(
        functools.partial(_na_kernel, rows=rows, rows_per_step=rows_per_step),
        out_shape=jax.ShapeDtypeStruct((t, NA_HEADS * HEAD_DIM), ACT_DTYPE),
        grid=(batch, pairs, n_q),
        in_specs=[pl.BlockSpec((tq, LANES), lambda b, h, i: (b * n_q + i, h)),
                  pl.BlockSpec((seq, LANES), lambda b, h, i: (b, h)),
                  pl.BlockSpec((seq, LANES), lambda b, h, i: (b, h)),
                  pl.BlockSpec((1,) + bias_tab.shape[1:], lambda b, h, i: (layer * pairs + h, 0, 0))],
        out_specs=pl.BlockSpec((tq, LANES), lambda b, h, i: (b * n_q + i, h)),
        compiler_params=_params("parallel", "parallel", "parallel"),
        name="neighbourhood_attention",
    )(qb, kb, vb, bias_tab)


def _shift_rows(a, prev_row, next_row):
    n = a.shape[0]
    ridx = lax.broadcasted_iota(jnp.int32, a.shape, 0)
    up = jnp.where(ridx == 0, prev_row, pltpu.roll(a, 1, 0))
    dn = jnp.where(ridx == n - 1, next_row, pltpu.roll(a, n - 1, 0))
    return up, dn


def _merge_xattn_kernel(x_ref, ya_ref, yb_ref, gb_ref, u_ref, up_ref, un_ref, gate_ref, cw_ref, wb_ref, wm_ref,
                        k_ref, v_ref, wq_ref, wo_ref, g_ref, b_ref, o_ref,
                        *, s_blocks, halo, alpha, d_model, n_sub):
    i = pl.program_id(0)
    has_prev = i % s_blocks != 0
    has_next = i % s_blocks != s_blocks - 1
    u = u_ref[...].astype(jnp.float32)
    prev_row = jnp.where(has_prev, up_ref[halo - 1:halo, :].astype(jnp.float32), 0.0)
    next_row = jnp.where(has_next, un_ref[0:1, :].astype(jnp.float32), 0.0)
    u_up, u_dn = _shift_rows(u, prev_row, next_row)
    cw = cw_ref[0]
    conv = u_up * cw[0:1, :] + u * cw[1:2, :] + u_dn * cw[2:3, :]
    yc = (gb_ref[...].astype(jnp.float32) * conv).astype(MXU_DTYPE)
    ln_g, ln_b = g_ref[0], b_ref[0]
    k = k_ref[...]
    v = v_ref[...]
    hd = d_model // XA_HEADS
    ts = x_ref.shape[0] // n_sub
    subs = [slice(sub * ts, (sub + 1) * ts) for sub in range(n_sub)]

    def merge(rows):
        gate = gate_ref[rows, :].astype(jnp.float32)
        merged = (gate[:, 0:d_model] * _dot(ya_ref[rows, :], wb_ref[0, 0])
                  + gate[:, d_model:2 * d_model] * _dot(yb_ref[rows, :], wb_ref[0, 1])
                  + gate[:, 2 * d_model:3 * d_model] * _dot(yc[rows, :], wb_ref[0, 2]))
        return alpha * x_ref[rows, :] + _dot(merged, wm_ref[0])

    def attend(q):
        outs = []
        for h in range(XA_HEADS):
            sl = slice(h * hd, (h + 1) * hd)
            s = _dot_nt(q[:, sl], k[:, sl])
            m = jnp.max(s, axis=1, keepdims=True)
            p = jnp.exp(s - m)
            l = jnp.sum(p, axis=1, keepdims=True)
            outs.append(_dot(p, v[:, sl]) / l)
        return jnp.concatenate(outs, axis=1)

    pre = [merge(rows) for rows in subs]
    x1 = [_layer_norm(a, ln_g[0:1, :], ln_b[0:1, :]) for a in pre]
    q = [_dot(a, wq_ref[0]) * (hd ** -0.5) for a in x1]
    att = [attend(a) for a in q]
    h_out = [_dot(a, wo_ref[0]) for a in att]
    for rows, a, b in zip(subs, x1, h_out):
        o_ref[rows, :] = _layer_norm(alpha * a + b, ln_g[1:2, :], ln_b[1:2, :])


def _merge_xattn(x2d, ya, yb, gb, u, gates, conv_w, w_branch, w_mix, kv, xa_q, xa_o, ln_g, ln_b,
                 layer, seq, mem_len, tm, alpha):
    t, d = x2d.shape
    halo = 16
    hb = tm // halo
    n_halo = t // halo
    s_blocks = seq // tm
    row = lambda i: (i, 0)
    const2 = lambda i: (0, 0)
    per_layer = lambda a: pl.BlockSpec((1,) + a.shape[1:], lambda i: (layer,) + (0,) * (a.ndim - 1))
    return pl.pallas_call(
        functools.partial(_merge_xattn_kernel, s_blocks=s_blocks, halo=halo, alpha=alpha, d_model=d,
                          n_sub=2 if tm % 32 == 0 else 1),
        out_shape=jax.ShapeDtypeStruct((t, d), jnp.float32),
        grid=(t // tm,),
        in_specs=[pl.BlockSpec((tm, d), row),
                  pl.BlockSpec((tm, SC_W), row),
                  pl.BlockSpec((tm, SC_W), row),
                  pl.BlockSpec((tm, SC_W), row),
                  pl.BlockSpec((tm, SC_W), row),
                  pl.BlockSpec((halo, SC_W), lambda i: (jnp.maximum(i * hb - 1, 0), 0)),
                  pl.BlockSpec((halo, SC_W), lambda i: (jnp.minimum((i + 1) * hb, n_halo - 1), 0)),
                  pl.BlockSpec((tm, N_BRANCH * d), row),
                  per_layer(conv_w),
                  _resident(w_branch, layer),
                  _resident(w_mix, layer),
                  pl.BlockSpec((mem_len, d), lambda i: (i // s_blocks, 0)),
                  pl.BlockSpec((mem_len, d), lambda i: (i // s_blocks, 1)),
                  _resident(xa_q, layer),
                  _resident(xa_o, layer),
                  per_layer(ln_g),
                  per_layer(ln_b)],
        out_specs=pl.BlockSpec((tm, d), row),
        compiler_params=_params("parallel"),
        name="merge_xattn",
    )(x2d, ya, yb, gb, u, u, u, gates, conv_w, w_branch, w_mix, kv, kv, xa_q, xa_o, ln_g, ln_b)


def _kv_kernel(m_ref, w_ref, o_ref):
    o_ref[...] = _dot(m_ref[...], w_ref[0]).astype(o_ref.dtype)


def _mem_kv(mem2d, xa_kv, layer):
    t, d = mem2d.shape
    n = xa_kv.shape[-1]
    tm = min(t, 256)
    return pl.pallas_call(
        _kv_kernel,
        out_shape=jax.ShapeDtypeStruct((t, n), ACT_DTYPE),
        grid=(t // tm,),
        in_specs=[pl.BlockSpec((tm, d), lambda i: (i, 0)), _resident(xa_kv, layer)],
        out_specs=pl.BlockSpec((tm, n), lambda i: (i, 0)),
        compiler_params=_params("parallel"),
        name="mem_kv",
    )(mem2d, xa_kv)


def _ffn_kernel(x_ref, xp_ref, xn_ref, wi_ref, cw_ref, cb_ref, wo_ref, g_ref, b_ref, o_ref, hid_ref,
                *, s_blocks, halo, alpha, d_ff, tf):
    i = pl.program_id(0)
    has_prev = i % s_blocks != 0
    has_next = i % s_blocks != s_blocks - 1
    x = x_ref[...]
    xb = x.astype(MXU_DTYPE)
    cw_all = cw_ref[0]
    ln_g, ln_b = g_ref[0], b_ref[0]
    xe = jnp.concatenate([jnp.where(has_prev, xp_ref[halo - 1:halo, :], 0.0),
                          jnp.where(has_next, xn_ref[0:1, :], 0.0),
                          jnp.zeros((halo - 2, x.shape[1]), jnp.float32)], axis=0).astype(MXU_DTYPE)
    for c in range(d_ff // tf):
        cols = slice(c * tf, (c + 1) * tf)
        w_c = jnp.concatenate([wi_ref[0, :, cols], wi_ref[0, :, d_ff + c * tf:d_ff + (c + 1) * tf]], axis=1)
        ug = jnp.dot(xb, w_c, preferred_element_type=jnp.float32)
        u, gt = ug[:, 0:tf], ug[:, tf:2 * tf]
        ge = jnp.dot(xe, w_c, preferred_element_type=jnp.float32)[:, tf:2 * tf]
        g_up, g_dn = _shift_rows(gt, ge[0:1, :], ge[1:2, :])
        cw = cw_all[:, cols]
        a = g_up * cw[0:1, :] + gt * cw[1:2, :] + g_dn * cw[2:3, :] + cb_ref[0, :, cols]
        hid_ref[:, cols] = ((a * jax.nn.sigmoid(a)) * u).astype(hid_ref.dtype)
    half = x.shape[0] // 2
    halves = (slice(0, half), slice(half, 2 * half))
    h = [jnp.dot(hid_ref[r, :], wo_ref[0], preferred_element_type=jnp.float32) for r in halves]
    for r, hr in zip(halves, h):
        o_ref[r, :] = _layer_norm(alpha * x[r, :] + hr, ln_g[2:3, :], ln_b[2:3, :])


def _conv_ffn(x2d, w_in, conv_w, conv_b, w_out, ln_g, ln_b, layer, seq, tm, alpha):
    t, d = x2d.shape
    d_ff = w_out.shape[1]
    tf = FFN_CHUNK
    halo = 8
    hb = tm // halo
    n_halo = t // halo
    row = lambda i: (i, 0)
    const2 = lambda i: (0, 0)
    per_layer = lambda a: pl.BlockSpec((1,) + a.shape[1:], lambda i: (layer,) + (0,) * (a.ndim - 1))
    return pl.pallas_call(
        functools.partial(_ffn_kernel, s_blocks=seq // tm, halo=halo, alpha=alpha, d_ff=d_ff, tf=tf),
        out_shape=jax.ShapeDtypeStruct((t, d), jnp.float32),
        grid=(t // tm,),
        in_specs=[pl.BlockSpec((tm, d), row),
                  pl.BlockSpec((halo, d), lambda i: (jnp.maximum(i * hb - 1, 0), 0)),
                  pl.BlockSpec((halo, d), lambda i: (jnp.minimum((i + 1) * hb, n_halo - 1), 0)),
                  _resident(w_in, layer),
                  per_layer(conv_w),
                  per_layer(conv_b),
                  _resident(w_out, layer),
                  per_layer(ln_g),
                  per_layer(ln_b)],
        out_specs=pl.BlockSpec((tm, d), row),
        scratch_shapes=[pltpu.VMEM((tm, d_ff), MXU_DTYPE)],
        compiler_params=_params("parallel"),
        name="conv_ffn",
    )(x2d, x2d, x2d, w_in, conv_w, conv_b, w_out, ln_g, ln_b)

def kernel(x, mem, emb_ln_g, emb_ln_b, w_in, lam_q1, lam_k1, lam_q2, lam_k2, subln_g, rpb, sc_conv_w, w_branch,
           w_mix_out, xa_q, xa_kv, xa_o, ffn_w_in, ffn_conv_w, ffn_conv_b, ffn_w_out, ln_g, ln_b):
    batch, seq, d = x.shape
    depth = w_in.shape[0]
    mem_len = mem.shape[1]
    assert seq % GRID_W == 0 and seq // GRID_W >= NA_WIN_ROWS
    alpha = (2.0 * depth) ** 0.25
    tm = min(512, seq)
    tm_ffn = min(1024, seq)
    tq = min(512, seq)
    tk = min(2048, seq // 2)
    w_in, w_branch, w_mix_out, xa_q, xa_kv, xa_o, ffn_w_in, ffn_w_out = (
        _to_mxu_dtype(w) for w in (w_in, w_branch, w_mix_out, xa_q, xa_kv, xa_o, ffn_w_in, ffn_w_out))

    rope_tab = _rope_table(seq)
    bias_tab = _na_bias_table(rpb.reshape((depth * NA_HEADS,) + rpb.shape[2:]))
    stack3 = lambda a: a.reshape(depth, 1, a.shape[-1])
    lq1, lk1, lq2, lk2, sub_g, conv_b = (stack3(a) for a in (lam_q1, lam_k1, lam_q2, lam_k2, subln_g, ffn_conv_b))
    h = x.reshape(batch * seq, d)
    mem2d = mem.reshape(batch * mem_len, d)
    for l in range(depth):
        lam_init = 0.8 - 0.6 * math.exp(-0.3 * l)
        if l == 0:
            h, *proj = _input_proj(h, w_in, l, rope_tab, seq, tm, emb_ln=(emb_ln_g, emb_ln_b))
        else:
            proj = _input_proj(h, w_in, l, rope_tab, seq, tm)
        qa, ka, va, qb, kb, vb, gb, u, gates = proj
        ya = _diff_attention(qa, ka, va, lq1, lk1, lq2, lk2, sub_g, l, lam_init, batch, seq, tq, tk)
        yb = _neighbourhood_attention(qb, kb, vb, bias_tab, l, batch, seq, NA_ROWS_PER_STEP)
        kv = _mem_kv(mem2d, xa_kv, l)
        h = _merge_xattn(h, ya, yb, gb, u, gates, sc_conv_w, w_branch, w_mix_out, kv, xa_q, xa_o,
                         ln_g, ln_b, l, seq, mem_len, tm, alpha)
        h = _conv_ffn(h, ffn_w_in, ffn_conv_w, conv_b, ffn_w_out, ln_g, ln_b, l, seq, tm_ffn, alpha)
    return h.reshape(batch, seq, d)
```
